```python
import math
import jax, jax.numpy as jnp
from jax import lax
import numpy as np

D_MODEL = 4096
BATCH = 2
SEQ = 8192
DEPTH = 2

N_MIXERS = 2
N_DIFF_LAYERS = (DEPTH + 1) // 2
N_MOBA_LAYERS = DEPTH // 2

ROPE_THETA = 500000.0

DIFF_HEADS = D_MODEL // 256
DIFF_QK_DIM = 128
DIFF_V_DIM = 2 * DIFF_QK_DIM
DIFF_Q_COLS = DIFF_HEADS * 2 * DIFF_QK_DIM
DIFF_V_COLS = DIFF_HEADS * DIFF_V_DIM
DENSE_Q_BLOCK = 128

MOBA_HEADS = D_MODEL // 128
MOBA_HEAD_DIM = 128
MOBA_COLS = MOBA_HEADS * MOBA_HEAD_DIM
MOBA_BLOCK = 256
MOBA_TOPK = 3
MOBA_Q_CHUNK = 16

ROT_DIM = DIFF_QK_DIM // 4

N_EXPERTS = 32
TOP_K = 4
D_EXPERT = 768
SWIGLU_LIMIT = 7.0
SWIGLU_ALPHA = 1.702
MOE_ROW_BLOCK = 128

LN_EPS = 1e-5
RMS_EPS = 1e-5
DEEPNORM_ALPHA = (2 * DEPTH) ** 0.25
DEEPNORM_BETA = (8 * DEPTH) ** -0.25

kernel_name = 'hybrid_diffattn_moba_moe_deepnorm'


def layer_norm(x, g, b):
    xf = x.astype(jnp.float32)
    mu = jnp.mean(xf, axis=-1, keepdims=True)
    var = jnp.mean(jnp.square(xf - mu), axis=-1, keepdims=True)
    y = (xf - mu) * lax.rsqrt(var + LN_EPS) * g.astype(jnp.float32) + b.astype(jnp.float32)
    return y.astype(x.dtype)


def rope_tables(positions):
    inv = 1.0 / (ROPE_THETA ** (jnp.arange(0, ROT_DIM, 2, dtype=jnp.float32) / ROT_DIM))
    ang = positions.astype(jnp.float32)[..., None] * inv
    return jnp.cos(ang), jnp.sin(ang)


def apply_partial_rope(x, cos, sin):
    half = ROT_DIM // 2
    c = cos.astype(x.dtype)
    s = sin.astype(x.dtype)
    x1 = x[..., :half]
    x2 = x[..., half:ROT_DIM]
    return jnp.concatenate([x1 * c - x2 * s, x2 * c + x1 * s, x[..., ROT_DIM:]], axis=-1)


def diff_attention(x, cos, sin, w_qkv, lq1, lk1, lq2, lk2, subln_g, w_o, lambda_init):
    B, S, _ = x.shape
    H, d = DIFF_HEADS, DIFF_QK_DIM
    qkv = x @ w_qkv
    q = qkv[..., :DIFF_Q_COLS].reshape(B, S, H, 2, d).transpose(0, 2, 3, 1, 4)
    k = qkv[..., DIFF_Q_COLS:2 * DIFF_Q_COLS].reshape(B, S, H, 2, d).transpose(0, 2, 3, 1, 4)
    v = qkv[..., 2 * DIFF_Q_COLS:].reshape(B, S, H, DIFF_V_DIM).transpose(0, 2, 1, 3)
    c5, s5 = cos[:, None, None], sin[:, None, None]
    q = apply_partial_rope(q, c5, s5)
    k = apply_partial_rope(k, c5, s5)
    lam = (jnp.exp(jnp.sum(lq1.astype(jnp.float32) * lk1.astype(jnp.float32)))
           - jnp.exp(jnp.sum(lq2.astype(jnp.float32) * lk2.astype(jnp.float32)))
           + lambda_init)
    scale = d ** -0.5
    kpos = jnp.arange(S)

    def q_block(i):
        qs = lax.dynamic_slice_in_dim(q, i * DENSE_Q_BLOCK, DENSE_Q_BLOCK, axis=3)
        sc = jnp.einsum('bhcqd,bhckd->bhcqk', qs, k).astype(jnp.float32) * scale
        qpos = i * DENSE_Q_BLOCK + jnp.arange(DENSE_Q_BLOCK)
        sc = jnp.where(kpos[None, :] <= qpos[:, None], sc, -jnp.inf)
        p = jax.nn.softmax(sc, axis=-1)
        a = p[:, :, 0] - lam * p[:, :, 1]
        return jnp.einsum('bhqk,bhkv->bhqv', a.astype(v.dtype), v)

    o = lax.map(q_block, jnp.arange(S // DENSE_Q_BLOCK))
    o = o.transpose(1, 2, 0, 3, 4).reshape(B, H, S, DIFF_V_DIM)
    of = o.astype(jnp.float32)
    of = of * lax.rsqrt(jnp.mean(jnp.square(of), axis=-1, keepdims=True) + RMS_EPS)
    of = of * subln_g.astype(jnp.float32) * (1.0 - lambda_init)
    o = of.astype(x.dtype).transpose(0, 2, 1, 3).reshape(B, S, DIFF_V_COLS)
    return o @ w_o


def moba_attention(x, cos, sin, w_qkv, w_o):
    B, S, _ = x.shape
    H, d, BLK, C = MOBA_HEADS, MOBA_HEAD_DIM, MOBA_BLOCK, MOBA_Q_CHUNK
    qkv = x @ w_qkv
    q = qkv[..., :MOBA_COLS].reshape(B, S, H, d).transpose(0, 2, 1, 3)
    k = qkv[..., MOBA_COLS:2 * MOBA_COLS].reshape(B, S, H, d).transpose(0, 2, 1, 3)
    v = qkv[..., 2 * MOBA_COLS:].reshape(B, S, H, d).transpose(0, 2, 1, 3)
    c4, s4 = cos[:, None], sin[:, None]
    q = apply_partial_rope(q, c4, s4)
    k = apply_partial_rope(k, c4, s4)
    pad = (-S) % BLK
    kb = jnp.pad(k, ((0, 0), (0, 0), (0, pad), (0, 0)))
    vb = jnp.pad(v, ((0, 0), (0, 0), (0, pad), (0, 0)))
    NB = (S + pad) // BLK
    kb = kb.reshape(B, H, NB, BLK, d)
    vb = vb.reshape(B, H, NB, BLK, d)
    kmean = jnp.mean(kb.astype(jnp.float32), axis=3)
    n_sel = min(MOBA_TOPK, NB)
    scale = d ** -0.5
    blk_ids = jnp.arange(NB)
    b_idx = jnp.arange(B)[:, None, None, None]
    h_idx = jnp.arange(H)[None, :, None, None]

    def q_chunk(c):
        start = c * C
        qc = lax.dynamic_slice_in_dim(q, start, C, axis=2)
        qpos = start + jnp.arange(C)
        own = start // BLK
        gate = jnp.einsum('bhqd,bhnd->bhqn', qc.astype(jnp.float32), kmean)
        gate = jnp.where(blk_ids < own, gate, -jnp.inf)
        _, sel = lax.top_k(gate, n_sel)
        sel_valid = jnp.arange(n_sel) < own
        ksel = kb[b_idx, h_idx, sel]
        vsel = vb[b_idx, h_idx, sel]
        s_sel = jnp.einsum('bhqd,bhqnkd->bhqnk', qc, ksel).astype(jnp.float32) * scale
        s_sel = jnp.where(sel_valid[:, None], s_sel, -jnp.inf).reshape(B, H, C, n_sel * BLK)
        k_own = lax.dynamic_index_in_dim(kb, own, axis=2, keepdims=False)
        v_own = lax.dynamic_index_in_dim(vb, own, axis=2, keepdims=False)
        s_own = jnp.einsum('bhqd,bhkd->bhqk', qc, k_own).astype(jnp.float32) * scale
        kpos_own = own * BLK + jnp.arange(BLK)
        s_own = jnp.where(kpos_own[None, :] <= qpos[:, None], s_own, -jnp.inf)
        p = jax.nn.softmax(jnp.concatenate([s_sel, s_own], axis=-1), axis=-1).astype(v.dtype)
        p_sel = p[..., :n_sel * BLK].reshape(B, H, C, n_sel, BLK)
        p_own = p[..., n_sel * BLK:]
        return (jnp.einsum('bhqnk,bhqnkd->bhqd', p_sel, vsel)
                + jnp.einsum('bhqk,bhkd->bhqd', p_own, v_own))

    o = lax.map(q_chunk, jnp.arange(S // C))
    o = o.transpose(1, 0, 3, 2, 4).reshape(B, S, MOBA_COLS)
    return o @ w_o


def moe_ffn(x, router_w, router_b, w_gate, b_gate, w_up, b_up, w_down, b_down):
    B, S, D = x.shape
    G = MOE_ROW_BLOCK
    xf = x.reshape(B * S, D)
    N = xf.shape[0]
    logits = (xf @ router_w + router_b).astype(jnp.float32)
    top_val, top_idx = lax.top_k(logits, TOP_K)
    top_w = jax.nn.softmax(top_val, axis=-1)
    NK = N * TOP_K
    e_flat = top_idx.reshape(NK)
    w_flat = top_w.reshape(NK)
    tok_flat = jnp.arange(NK, dtype=jnp.int32) // TOP_K
    counts = jnp.bincount(e_flat, length=N_EXPERTS)
    padded = (counts + G - 1) // G * G
    pend = jnp.cumsum(padded)
    pstart = pend - padded
    ustart = jnp.cumsum(counts) - counts
    order = jnp.argsort(e_flat)
    e_sorted = e_flat[order]
    dest = pstart[e_sorted] + jnp.arange(NK) - ustart[e_sorted]
    n_blocks = -(-NK // G) + N_EXPERTS
    P = n_blocks * G
    row_tok = jnp.zeros((P,), jnp.int32).at[dest].set(tok_flat[order])
    row_w = jnp.zeros((P,), jnp.float32).at[dest].set(w_flat[order])
    block_exp = jnp.minimum(
        jnp.searchsorted(pend, jnp.arange(n_blocks) * G, side='right'), N_EXPERTS - 1)

    def expert_block(args):
        toks, wts, e = args
        xb = xf[toks]
        g = jnp.minimum(xb @ w_gate[e] + b_gate[e], SWIGLU_LIMIT)
        u = jnp.clip(xb @ w_up[e] + b_up[e], -SWIGLU_LIMIT, SWIGLU_LIMIT)
        h = g * jax.nn.sigmoid(SWIGLU_ALPHA * g) * (u + 1.0)
        y = h @ w_down[e] + b_down[e]
        return y * wts[:, None].astype(y.dtype)

    ys = lax.map(expert_block, (row_tok.reshape(n_blocks, G), row_w.reshape(n_blocks, G), block_exp))
    out = jax.ops.segment_sum(ys.reshape(P, D), row_tok, num_segments=N)
    return out.reshape(B, S, D)


def setup_inputs(seed: int = 0) -> dict:
    key = jax.random.key(seed)
    ks = jax.random.split(key, 24)
    D, E, F = D_MODEL, N_EXPERTS, D_EXPERT

    def nrm(k, shape, scale):
        return jax.random.normal(k, shape, jnp.float32) * scale

    x = nrm(ks[0], (BATCH, SEQ, D), 1.0)
    offsets = jax.random.randint(ks[1], (BATCH, 1), 0, 4096, dtype=jnp.int32)
    positions = offsets + jnp.arange(SEQ, dtype=jnp.int32)[None, :]
    return {
        'x': x,
        'positions': positions,
        'diff_w_qkv': nrm(ks[2], (N_DIFF_LAYERS, D, 2 * DIFF_Q_COLS + DIFF_V_COLS), D ** -0.5),
        'diff_lambda_q1': nrm(ks[3], (N_DIFF_LAYERS, DIFF_QK_DIM), 0.1),
        'diff_lambda_k1': nrm(ks[4], (N_DIFF_LAYERS, DIFF_QK_DIM), 0.1),
        'diff_lambda_q2': nrm(ks[5], (N_DIFF_LAYERS, DIFF_QK_DIM), 0.1),
        'diff_lambda_k2': nrm(ks[6], (N_DIFF_LAYERS, DIFF_QK_DIM), 0.1),
        'diff_subln_g': 1.0 + nrm(ks[7], (N_DIFF_LAYERS, DIFF_V_DIM), 0.02),
        'diff_w_o': nrm(ks[8], (N_DIFF_LAYERS, DIFF_V_COLS, D), DIFF_V_COLS ** -0.5 * DEEPNORM_BETA),
        'moba_w_qkv': nrm(ks[9], (N_MOBA_LAYERS, D, 3 * MOBA_COLS), D ** -0.5),
        'moba_w_o': nrm(ks[10], (N_MOBA_LAYERS, MOBA_COLS, D), MOBA_COLS ** -0.5 * DEEPNORM_BETA),
        'ln_mix_g': 1.0 + nrm(ks[11], (DEPTH, D), 0.02),
        'ln_mix_b': nrm(ks[12], (DEPTH, D), 0.02),
        'router_w': nrm(ks[13], (DEPTH, D, E), D ** -0.5),
        'router_b': nrm(ks[14], (DEPTH, E), 0.01),
        'exp_w_gate': nrm(ks[15], (DEPTH, E, D, F), D ** -0.5),
        'exp_b_gate': nrm(ks[16], (DEPTH, E, F), 0.01),
        'exp_w_up': nrm(ks[17], (DEPTH, E, D, F), D ** -0.5),
        'exp_b_up': nrm(ks[18], (DEPTH, E, F), 0.01),
        'exp_w_down': nrm(ks[19], (DEPTH, E, F, D), F ** -0.5 * DEEPNORM_BETA),
        'exp_b_down': nrm(ks[20], (DEPTH, E, D), 0.01),
        'ln_ffn_g': 1.0 + nrm(ks[21], (DEPTH, D), 0.02),
        'ln_ffn_b': nrm(ks[22], (DEPTH, D), 0.02),
    }


def reference(x, positions, diff_w_qkv, diff_lambda_q1, diff_lambda_k1, diff_lambda_q2,
              diff_lambda_k2, diff_subln_g, diff_w_o, moba_w_qkv, moba_w_o, ln_mix_g, ln_mix_b,
              router_w, router_b, exp_w_gate, exp_b_gate, exp_w_up, exp_b_up, exp_w_down,
              exp_b_down, ln_ffn_g, ln_ffn_b):
    cos, sin = rope_tables(positions)
    for i in range(DEPTH):
        j = i // N_MIXERS
        if i % N_MIXERS == 0:
            lambda_init = 0.8 - 0.6 * math.exp(-0.3 * i)
            mix = diff_attention(x, cos, sin, diff_w_qkv[j], diff_lambda_q1[j], diff_lambda_k1[j],
                                 diff_lambda_q2[j], diff_lambda_k2[j], diff_subln_g[j],
                                 diff_w_o[j], lambda_init)
        else:
            mix = moba_attention(x, cos, sin, moba_w_qkv[j], moba_w_o[j])
        x = layer_norm(DEEPNORM_ALPHA * x + mix, ln_mix_g[i], ln_mix_b[i])
        ffn = moe_ffn(x, router_w[i], router_b[i], exp_w_gate[i], exp_b_gate[i], exp_w_up[i],
                      exp_b_up[i], exp_w_down[i], exp_b_down[i])
        x = layer_norm(DEEPNORM_ALPHA * x + ffn, ln_ffn_g[i], ln_ffn_b[i])
    return x
```

```python
import functools
import math

import jax
import jax.numpy as jnp
from jax import lax
from jax.experimental import pallas as pl
from jax.experimental.pallas import tpu as pltpu

DEPTH = 2
ROPE_THETA = 500000.0
HEAD_DIM = 128
ROT_DIM = HEAD_DIM // 4
ROT_HALF = ROT_DIM // 2
DIFF_V_DIM = 2 * HEAD_DIM
MOBA_BLOCK = 256
MOBA_TOPK = 3
TOP_K = 4
SWIGLU_LIMIT = 7.0
SWIGLU_ALPHA = 1.702
LN_EPS = 1e-5
RMS_EPS = 1e-5
DEEPNORM_ALPHA = (2 * DEPTH) ** 0.25

LANES = 128
NEG_BIG = -1e30
VMEM_LIMIT = 56 * 1024 * 1024


def _cparams(*sem):
    return pltpu.CompilerParams(dimension_semantics=sem, vmem_limit_bytes=VMEM_LIMIT)


def _pick(n, pref):
    t = min(pref, n)
    while n % t:
        t //= 2
    return t


def _qkv_kernel(x_ref, w_ref, c_ref, s_ref, o_ref, *, n_rope_tiles):
    j = pl.program_id(0)
    acc = jnp.dot(x_ref[...], w_ref[...], preferred_element_type=jnp.float32)
    tn = acc.shape[1]

    @pl.when(j < n_rope_tiles)
    def _():
        c = c_ref[...]
        s = s_ref[...]
        lane = lax.broadcasted_iota(jnp.int32, c.shape, 1)
        for h in range(tn // LANES):
            a = acc[:, h * LANES:(h + 1) * LANES]
            partner = jnp.where(lane < ROT_HALF,
                                pltpu.roll(a, LANES - ROT_HALF, 1),
                                pltpu.roll(a, ROT_HALF, 1))
            o_ref[:, h * LANES:(h + 1) * LANES] = (a * c + partner * s).astype(o_ref.dtype)

    @pl.when(j >= n_rope_tiles)
    def _():
        o_ref[...] = acc.astype(o_ref.dtype)


def qkv_proj(x_bf, w_bf, rope_c, rope_s, n_rope_cols):
    m, k = x_bf.shape
    n = w_bf.shape[1]
    tm = _pick(m, 512)
    tn = _pick(math.gcd(n, n_rope_cols), 1024)
    return pl.pallas_call(
        functools.partial(_qkv_kernel, n_rope_tiles=n_rope_cols // tn),
        grid=(n // tn, m // tm),
        in_specs=[
            pl.BlockSpec((tm, k), lambda j, i: (i, 0)),
            pl.BlockSpec((k, tn), lambda j, i: (0, j)),
            pl.BlockSpec((tm, LANES), lambda j, i: (i, 0)),
            pl.BlockSpec((tm, LANES), lambda j, i: (i, 0)),
        ],
        out_specs=pl.BlockSpec((tm, tn), lambda j, i: (i, j)),
        out_shape=jax.ShapeDtypeStruct((m, n), jnp.bfloat16),
        compiler_params=_cparams("parallel", "arbitrary"),
        name="qkv_proj",
    )(x_bf, w_bf, rope_c, rope_s)


def _diff_attn_kernel(lq1_ref, lk1_ref, lq2_ref, lk2_ref, g_ref, q_ref, k_ref, v_ref,
                      o_ref, acc1, acc2, *, tq, lambda_init):
    i = pl.program_id(2)
    d = HEAD_DIM
    scale = d ** -0.5
    q = q_ref[0]
    qs = (q[:, :d], q[:, d:])
    accs = (acc1, acc2)
    dn = (((1,), (1,)), ((), ()))

    def step(j, carry, masked):
        kj = k_ref[0, pl.ds(j * tq, tq), :]
        vj = v_ref[0, pl.ds(j * tq, tq), :]
        new = []
        for c in range(2):
            m_old, l_old = carry[2 * c], carry[2 * c + 1]
            s = lax.dot_general(qs[c], kj[:, c * d:(c + 1) * d], dn,
                                preferred_element_type=jnp.float32) * scale
            if masked:
                row = lax.broadcasted_iota(jnp.int32, s.shape, 0)
                col = lax.broadcasted_iota(jnp.int32, s.shape, 1)
                s = jnp.where(col <= row, s, NEG_BIG)
            m_new = jnp.maximum(m_old, jnp.max(s, axis=1, keepdims=True))
            alpha = jnp.exp(m_old - m_new)
            p = jnp.exp(s - m_new)
            l_new = alpha * l_old + jnp.sum(p, axis=1, keepdims=True)
            accs[c][...] = alpha * accs[c][...] + jnp.dot(
                p.astype(vj.dtype), vj, preferred_element_type=jnp.float32)
            new += [m_new, l_new]
        return tuple(new)

    acc1[...] = jnp.zeros_like(acc1)
    acc2[...] = jnp.zeros_like(acc2)
    m0 = jnp.full((tq, 1), NEG_BIG, jnp.float32)
    l0 = jnp.zeros((tq, 1), jnp.float32)
    carry = lax.fori_loop(0, i, lambda j, c: step(j, c, False), (m0, l0, m0, l0))
    m1, l1, m2, l2 = step(i, carry, True)

    lam = (jnp.exp(jnp.sum(lq1_ref[...] * lk1_ref[...], axis=1, keepdims=True))
           - jnp.exp(jnp.sum(lq2_ref[...] * lk2_ref[...], axis=1, keepdims=True))
           + lambda_init)
    o = acc1[...] / l1 - lam * (acc2[...] / l2)
    o = o * lax.rsqrt(jnp.mean(o * o, axis=1, keepdims=True) + RMS_EPS)
    o = o * g_ref[...] * (1.0 - lambda_init)
    o_ref[0] = o.astype(o_ref.dtype)


def diff_attention(qkv, lq1, lk1, lq2, lk2, subln_g, n_heads, lambda_init):
    b, s, _ = qkv.shape
    dv = DIFF_V_DIM
    tq = _pick(s, 512)
    vec = lambda a: a.reshape(1, -1).astype(jnp.float32)
    small = lambda n: pl.BlockSpec((1, n), lambda bi, h, i: (0, 0))
    return pl.pallas_call(
        functools.partial(_diff_attn_kernel, tq=tq, lambda_init=lambda_init),
        grid=(b, n_heads, s // tq),
        in_specs=[
            small(HEAD_DIM), small(HEAD_DIM), small(HEAD_DIM), small(HEAD_DIM), small(dv),
            pl.BlockSpec((1, tq, dv), lambda bi, h, i: (bi, i, h)),
            pl.BlockSpec((1, s, dv), lambda bi, h, i: (bi, 0, n_heads + h)),
            pl.BlockSpec((1, s, dv), lambda bi, h, i: (bi, 0, 2 * n_heads + h)),
        ],
        out_specs=pl.BlockSpec((1, tq, dv), lambda bi, h, i: (bi, i, h)),
        out_shape=jax.ShapeDtypeStruct((b, s, n_heads * dv), jnp.bfloat16),
        scratch_shapes=[pltpu.VMEM((tq, dv), jnp.float32), pltpu.VMEM((tq, dv), jnp.float32)],
        compiler_params=_cparams("parallel", "parallel", "arbitrary"),
        name="diff_attention",
    )(vec(lq1), vec(lk1), vec(lq2), vec(lk2), vec(subln_g), qkv, qkv, qkv)


def _moba_kernel(q_ref, k_ref, v_ref, o_ref, kmean, acc, *, nb):
    i = pl.program_id(2)
    blk = MOBA_BLOCK
    d = HEAD_DIM
    scale = d ** -0.5
    s_len = nb * blk
    dn = (((1,), (1,)), ((), ()))

    @pl.when(i == 0)
    def _():
        r = lax.broadcasted_iota(jnp.int32, (nb, s_len), 0)
        c = lax.broadcasted_iota(jnp.int32, (nb, s_len), 1)
        ind = jnp.where((c >= r * blk) & (c < (r + 1) * blk), 1.0, 0.0).astype(jnp.bfloat16)
        km = jnp.dot(ind, k_ref[0], preferred_element_type=jnp.float32) * (1.0 / blk)
        hi = km.astype(jnp.bfloat16)
        lo = (km - hi.astype(jnp.float32)).astype(jnp.bfloat16)
        kmean[0:nb, :] = hi
        kmean[nb:2 * nb, :] = lo

    q = q_ref[0]
    g2 = lax.dot_general(q, kmean[...], dn, preferred_element_type=jnp.float32)
    gate = g2[:, :nb] + g2[:, nb:]
    colf = lax.broadcasted_iota(jnp.int32, gate.shape, 1).astype(jnp.float32)
    i_f = i.astype(jnp.float32)
    gate = jnp.where(colf < i_f, gate, -jnp.inf)
    sel = jnp.zeros(gate.shape, jnp.float32)
    for _ in range(MOBA_TOPK):
        mx = jnp.max(gate, axis=1, keepdims=True)
        first = jnp.min(jnp.where(gate == mx, colf, float(nb)), axis=1, keepdims=True)
        hit = (colf == first) & (mx > -jnp.inf)
        sel = jnp.where(hit, 1.0, sel)
        gate = jnp.where(colf == first, -jnp.inf, gate)

    def scores(j):
        kj = k_ref[0, pl.ds(j * blk, blk), :]
        return lax.dot_general(q, kj, dn, preferred_element_type=jnp.float32) * scale

    s = scores(i)
    row = lax.broadcasted_iota(jnp.int32, s.shape, 0)
    col = lax.broadcasted_iota(jnp.int32, s.shape, 1)
    s = jnp.where(col <= row, s, NEG_BIG)
    m0 = jnp.max(s, axis=1, keepdims=True)
    p = jnp.exp(s - m0)
    l0 = jnp.sum(p, axis=1, keepdims=True)
    v_own = v_ref[0, pl.ds(i * blk, blk), :]
    acc[...] = jnp.dot(p.astype(v_own.dtype), v_own, preferred_element_type=jnp.float32)

    def step(j, carry):
        m_old, l_old = carry
        chosen = jnp.sum(jnp.where(colf == j.astype(jnp.float32), sel, 0.0),
                         axis=1, keepdims=True)
        sj = jnp.where(chosen > 0.0, scores(j), NEG_BIG)
        m_new = jnp.maximum(m_old, jnp.max(sj, axis=1, keepdims=True))
        alpha = jnp.exp(m_old - m_new)
        pj = jnp.exp(sj - m_new)
        l_new = alpha * l_old + jnp.sum(pj, axis=1, keepdims=True)
        vj = v_ref[0, pl.ds(j * blk, blk), :]
        acc[...] = alpha * acc[...] + jnp.dot(pj.astype(vj.dtype), vj,
                                              preferred_element_type=jnp.float32)
        return m_new, l_new

    _, l_fin = lax.fori_loop(0, i, step, (m0, l0))
    o_ref[0] = (acc[...] / l_fin).astype(o_ref.dtype)


def moba_attention(qkv, n_heads):
    b, s, _ = qkv.shape
    d = HEAD_DIM
    assert s % MOBA_BLOCK == 0
    nb = s // MOBA_BLOCK
    return pl.pallas_call(
        functools.partial(_moba_kernel, nb=nb),
        grid=(b, n_heads, nb),
        in_specs=[
            pl.BlockSpec((1, MOBA_BLOCK, d), lambda bi, h, i: (bi, i, h)),
            pl.BlockSpec((1, s, d), lambda bi, h, i: (bi, 0, n_heads + h)),
            pl.BlockSpec((1, s, d), lambda bi, h, i: (bi, 0, 2 * n_heads + h)),
        ],
        out_specs=pl.BlockSpec((1, MOBA_BLOCK, d), lambda bi, h, i: (bi, i, h)),
        out_shape=jax.ShapeDtypeStruct((b, s, n_heads * d), jnp.bfloat16),
        scratch_shapes=[pltpu.VMEM((2 * nb, d), jnp.bfloat16),
                        pltpu.VMEM((MOBA_BLOCK, d), jnp.float32)],
        compiler_params=_cparams("parallel", "parallel", "arbitrary"),
        name="moba_attention",
    )(qkv, qkv, qkv)


def _proj_resid_kernel(a_ref, w_ref, r_ref, o_ref):
    o_ref[...] = DEEPNORM_ALPHA * r_ref[...] + jnp.dot(
        a_ref[...], w_ref[...], preferred_element_type=jnp.float32)


def proj_resid(a_bf, w_bf, resid):
    m, k = a_bf.shape
    n = w_bf.shape[1]
    tm = _pick(m, 512)
    tn = _pick(n, 1024)
    return pl.pallas_call(
        _proj_resid_kernel,
        grid=(n // tn, m // tm),
        in_specs=[
            pl.BlockSpec((tm, k), lambda j, i: (i, 0)),
            pl.BlockSpec((k, tn), lambda j, i: (0, j)),
            pl.BlockSpec((tm, tn), lambda j, i: (i, j)),
        ],
        out_specs=pl.BlockSpec((tm, tn), lambda j, i: (i, j)),
        out_shape=jax.ShapeDtypeStruct((m, n), jnp.float32),
        compiler_params=_cparams("parallel", "arbitrary"),
        name="proj_resid",
    )(a_bf, w_bf, resid)


def _layer_norm_rows(z, g, b):
    mu = jnp.mean(z, axis=1, keepdims=True)
    zc = z - mu
    var = jnp.mean(zc * zc, axis=1, keepdims=True)
    return zc * lax.rsqrt(var + LN_EPS) * g + b


def _ln_router_kernel(z_ref, g_ref, b_ref, rw_ref, rb_ref,
                      y_ref, idx_ref, wt_ref, rank_ref, cnt_ref, carry):
    t = pl.program_id(0)

    @pl.when(t == 0)
    def _():
        carry[...] = jnp.zeros_like(carry)

    y = _layer_norm_rows(z_ref[...], g_ref[...], b_ref[...])
    y_ref[...] = y
    logits = jnp.dot(y, rw_ref[...], preferred_element_type=jnp.float32,
                     precision=lax.Precision.HIGHEST) + rb_ref[...]
    tm = logits.shape[0]
    lanef = lax.broadcasted_iota(jnp.int32, logits.shape, 1).astype(jnp.float32)
    vals, idxs = [], []
    onehot = jnp.zeros(logits.shape, jnp.float32)
    for _ in range(TOP_K):
        mx = jnp.max(logits, axis=1, keepdims=True)
        first = jnp.min(jnp.where(logits == mx, lanef, float(LANES)), axis=1, keepdims=True)
        hit = lanef == first
        onehot = jnp.where(hit, 1.0, onehot)
        logits = jnp.where(hit, -jnp.inf, logits)
        vals.append(mx)
        idxs.append(first)
    es = [jnp.exp(v - vals[0]) for v in vals]
    den = es[0] + es[1] + es[2] + es[3]

    r = lax.broadcasted_iota(jnp.int32, (tm, tm), 0)
    c = lax.broadcasted_iota(jnp.int32, (tm, tm), 1)
    tri = jnp.where(c < r, 1.0, 0.0).astype(jnp.bfloat16)
    before = jnp.dot(tri, onehot.astype(jnp.bfloat16),
                     preferred_element_type=jnp.float32) + carry[...]
    for k in range(TOP_K):
        rk = jnp.sum(jnp.where(lanef == idxs[k], before, 0.0), axis=1, keepdims=True)
        rank_ref[:, k:k + 1] = rk.astype(jnp.int32)
        idx_ref[:, k:k + 1] = idxs[k].astype(jnp.int32)
        wt_ref[:, k:k + 1] = es[k] / den
    carry[...] = carry[...] + jnp.sum(onehot, axis=0, keepdims=True)
    cnt_ref[...] = carry[...].astype(jnp.int32)


def ln_router(z, g, b, router_w, router_b):
    n, dm = z.shape
    e = router_w.shape[1]
    assert TOP_K <= e <= LANES
    rw = jnp.zeros((dm, LANES), jnp.float32).at[:, :e].set(router_w.astype(jnp.float32))
    rb = jnp.full((1, LANES), NEG_BIG, jnp.float32).at[0, :e].set(router_b.astype(jnp.float32))
    tm = _pick(n, 256)
    row = lambda w: pl.BlockSpec((tm, w), lambda t: (t, 0))
    const = lambda r, w: pl.BlockSpec((r, w), lambda t: (0, 0))
    return pl.pallas_call(
        _ln_router_kernel,
        grid=(n // tm,),
        in_specs=[row(dm), const(1, dm), const(1, dm), const(dm, LANES), const(1, LANES)],
        out_specs=[row(dm), row(TOP_K), row(TOP_K), row(TOP_K), const(1, LANES)],
        out_shape=[
            jax.ShapeDtypeStruct((n, dm), jnp.float32),
            jax.ShapeDtypeStruct((n, TOP_K), jnp.int32),
            jax.ShapeDtypeStruct((n, TOP_K), jnp.float32),
            jax.ShapeDtypeStruct((n, TOP_K), jnp.int32),
            jax.ShapeDtypeStruct((1, LANES), jnp.int32),
        ],
        scratch_shapes=[pltpu.VMEM((1, LANES), jnp.float32)],
        compiler_params=_cparams("arbitrary"),
        name="ln_router",
    )(z, g.reshape(1, dm), b.reshape(1, dm), rw, rb)


def _gather_rows_kernel(tok_ref, x_hbm, o_ref, sem, *, tg):
    base = pl.program_id(0) * tg

    def copy(r):
        return pltpu.make_async_copy(x_hbm.at[pl.ds(tok_ref[base + r], 1)],
                                     o_ref.at[pl.ds(r, 1)], sem)

    def start(r, _):
        copy(r).start()
        return 0

    def wait(r, _):
        copy(r).wait()
        return 0

    lax.fori_loop(0, tg, start, 0)
    lax.fori_loop(0, tg, wait, 0)


def gather_rows(x, row_tok, tg):
    p = row_tok.shape[0]
    dm = x.shape[1]
    return pl.pallas_call(
        functools.partial(_gather_rows_kernel, tg=tg),
        grid_spec=pltpu.PrefetchScalarGridSpec(
            num_scalar_prefetch=1,
            grid=(p // tg,),
            in_specs=[pl.BlockSpec(memory_space=pl.ANY)],
            out_specs=pl.BlockSpec((tg, dm), lambda t, tok: (t, 0)),
            scratch_shapes=[pltpu.SemaphoreType.DMA(())],
        ),
        out_shape=jax.ShapeDtypeStruct((p, dm), x.dtype),
        compiler_params=_cparams("arbitrary"),
        name="gather_rows",
    )(row_tok, x)


def _expert_up_kernel(be_ref, x_ref, wg_ref, bg_ref, wu_ref, bu_ref, h_ref):
    x = x_ref[...].astype(jnp.bfloat16)
    g = jnp.dot(x, wg_ref[0], preferred_element_type=jnp.float32) + bg_ref[0]
    u = jnp.dot(x, wu_ref[0], preferred_element_type=jnp.float32) + bu_ref[0]
    g = jnp.minimum(g, SWIGLU_LIMIT)
    u = jnp.clip(u, -SWIGLU_LIMIT, SWIGLU_LIMIT)
    h = g * jax.nn.sigmoid(SWIGLU_ALPHA * g) * (u + 1.0)
    h_ref[...] = h.astype(h_ref.dtype)


def _expert_down_kernel(be_ref, h_ref, wd_ref, bd_ref, y_ref):
    y_ref[...] = jnp.dot(h_ref[...], wd_ref[0], preferred_element_type=jnp.float32) + bd_ref[0]


def expert_ffn(xs, block_exp, wg, bg, wu, bu, wd, bd, tm):
    p, dm = xs.shape
    e, _, f = wg.shape
    nblk = p // tm
    rows = lambda w: pl.BlockSpec((tm, w), lambda t, be: (t, 0))
    per_exp = lambda r, c: pl.BlockSpec((1, r, c), lambda t, be: (be[t], 0, 0))
    h = pl.pallas_call(
        _expert_up_kernel,
        grid_spec=pltpu.PrefetchScalarGridSpec(
            num_scalar_prefetch=1,
            grid=(nblk,),
            in_specs=[rows(dm), per_exp(dm, f), per_exp(1, f), per_exp(dm, f), per_exp(1, f)],
            out_specs=rows(f),
        ),
        out_shape=jax.ShapeDtypeStruct((p, f), jnp.bfloat16),
        compiler_params=_cparams("arbitrary"),
        name="expert_up",
    )(block_exp, xs, wg, bg.reshape(e, 1, f), wu, bu.reshape(e, 1, f))
    return pl.pallas_call(
        _expert_down_kernel,
        grid_spec=pltpu.PrefetchScalarGridSpec(
            num_scalar_prefetch=1,
            grid=(nblk,),
            in_specs=[rows(f), per_exp(f, dm), per_exp(1, dm)],
            out_specs=rows(dm),
        ),
        out_shape=jax.ShapeDtypeStruct((p, dm), jnp.float32),
        compiler_params=_cparams("arbitrary"),
        name="expert_down",
    )(block_exp, h, wd, bd.reshape(e, 1, dm))


def _combine_ln_kernel(dest_ref, ys_hbm, wt_ref, x_ref, g_ref, b_ref, o_ref, obf_ref,
                       buf, sem, *, tc):
    base = pl.program_id(0) * (tc * TOP_K)

    def copy(r, k):
        return pltpu.make_async_copy(ys_hbm.at[pl.ds(dest_ref[base + r * TOP_K + k], 1)],
                                     buf.at[k, pl.ds(r, 1)], sem)

    def start(r, _):
        for k in range(TOP_K):
            copy(r, k).start()
        return 0

    def wait(r, _):
        for k in range(TOP_K):
            copy(r, k).wait()
        return 0

    lax.fori_loop(0, tc, start, 0)
    lax.fori_loop(0, tc, wait, 0)
    z = DEEPNORM_ALPHA * x_ref[...]
    for k in range(TOP_K):
        z = z + wt_ref[:, k:k + 1] * buf[k]
    y = _layer_norm_rows(z, g_ref[...], b_ref[...])
    o_ref[...] = y
    obf_ref[...] = y.astype(obf_ref.dtype)


def combine_ln(ys, dest, wt, x, g, b):
    n, dm = x.shape
    tc = _pick(n, 128)
    row = lambda w: pl.BlockSpec((tc, w), lambda t, d: (t, 0))
    const = lambda w: pl.BlockSpec((1, w), lambda t, d: (0, 0))
    return pl.pallas_call(
        functools.partial(_combine_ln_kernel, tc=tc),
        grid_spec=pltpu.PrefetchScalarGridSpec(
            num_scalar_prefetch=1,
            grid=(n // tc,),
            in_specs=[pl.BlockSpec(memory_space=pl.ANY), row(TOP_K), row(dm), const(dm), const(dm)],
            out_specs=[row(dm), row(dm)],
            scratch_shapes=[pltpu.VMEM((TOP_K, tc, dm), jnp.float32),
                            pltpu.SemaphoreType.DMA(())],
        ),
        out_shape=[jax.ShapeDtypeStruct((n, dm), jnp.float32),
                   jax.ShapeDtypeStruct((n, dm), jnp.bfloat16)],
        compiler_params=_cparams("arbitrary"),
        name="combine_ln",
    )(dest.reshape(-1), ys, wt, x, g.reshape(1, dm), b.reshape(1, dm))


def _rope_tables(positions):
    inv = 1.0 / (ROPE_THETA ** (jnp.arange(0, ROT_DIM, 2, dtype=jnp.float32) / ROT_DIM))
    ang = positions.astype(jnp.float32).reshape(-1, 1) * inv
    cos, sin = jnp.cos(ang), jnp.sin(ang)
    n = ang.shape[0]
    rest = LANES - ROT_DIM
    c = jnp.concatenate([cos, cos, jnp.ones((n, rest), jnp.float32)], axis=1)
    s = jnp.concatenate([-sin, sin, jnp.zeros((n, rest), jnp.float32)], axis=1)
    return c, s


def moe_ln(z, ln_g, ln_b, router_w, router_b, wg, bg, wu, bu, wd, bd, ln2_g, ln2_b, tm=256):
    n, dm = z.shape
    e = router_w.shape[1]
    x1, idx, wt, rank, cnt = ln_router(z, ln_g, ln_b, router_w, router_b)
    counts = cnt[0, :e]
    padded = (counts + tm - 1) // tm * tm
    pend = jnp.cumsum(padded)
    pstart = pend - padded
    dest = pstart[idx] + rank
    nblk = (n * TOP_K) // tm + e
    tok = jnp.broadcast_to(jnp.arange(n, dtype=jnp.int32)[:, None], (n, TOP_K))
    row_tok = jnp.zeros((nblk * tm,), jnp.int32).at[dest.reshape(-1)].set(tok.reshape(-1))
    block_exp = jnp.minimum(
        jnp.searchsorted(pend, jnp.arange(nblk, dtype=jnp.int32) * tm, side='right'),
        e - 1).astype(jnp.int32)
    xs = gather_rows(x1, row_tok, tm)
    ys = expert_ffn(xs, block_exp, wg, bg, wu, bu, wd, bd, tm)
    return combine_ln(ys, dest.astype(jnp.int32), wt, x1, ln2_g, ln2_b)


def kernel(x, positions, diff_w_qkv, diff_lambda_q1, diff_lambda_k1, diff_lambda_q2, diff_lambda_k2, diff_subln_g, diff_w_o, moba_w_qkv, moba_w_o, ln_mix_g, ln_mix_b, router_w, router_b, exp_w_gate, exp_b_gate, exp_w_up, exp_b_up, exp_w_down, exp_b_down, ln_ffn_g, ln_ffn_b):
    b, s, dm = x.shape
    n = b * s
    bf = jnp.bfloat16
    rope_c, rope_s = _rope_tables(positions)
    xf = x.reshape(n, dm)
    x_bf = xf.astype(bf)
    n_mixers = 2
    for i in range(DEPTH):
        j = i // n_mixers
        if i % n_mixers == 0:
            heads = dm // DIFF_V_DIM
            lambda_init = 0.8 - 0.6 * math.exp(-0.3 * i)
            qkv = qkv_proj(x_bf, diff_w_qkv[j].astype(bf), rope_c, rope_s, 2 * heads * DIFF_V_DIM)
            att = diff_attention(qkv.reshape(b, s, -1), diff_lambda_q1[j], diff_lambda_k1[j],
                                 diff_lambda_q2[j], diff_lambda_k2[j], diff_subln_g[j],
                                 heads, lambda_init)
            w_o = diff_w_o[j]
        else:
            heads = dm // HEAD_DIM
            qkv = qkv_proj(x_bf, moba_w_qkv[j].astype(bf), rope_c, rope_s, 2 * heads * HEAD_DIM)
            att = moba_attention(qkv.reshape(b, s, -1), heads)
            w_o = moba_w_o[j]
        z = proj_resid(att.reshape(n, -1), w_o.astype(bf), xf)
        xf, x_bf = moe_ln(z, ln_mix_g[i], ln_mix_b[i], router_w[i], router_b[i],
                          exp_w_gate[i].astype(bf), exp_b_gate[i], exp_w_up[i].astype(bf),
                          exp_b_up[i], exp_w_down[i].astype(bf), exp_b_down[i],
                          ln_ffn_g[i], ln_ffn_b[i])
    return xf.reshape(b, s, dm)
```

```python
import functools
import math

import jax
import jax.numpy as jnp
from jax import lax
from jax.experimental import pallas as pl
from jax.experimental.pallas import tpu as pltpu

DEPTH = 2
ROPE_THETA = 500000.0
HEAD_DIM = 128
ROT_DIM = HEAD_DIM // 4
ROT_HALF = ROT_DIM // 2
DIFF_V_DIM = 2 * HEAD_DIM
MOBA_BLOCK = 256
MOBA_TOPK = 3
TOP_K = 4
SWIGLU_LIMIT = 7.0
SWIGLU_ALPHA = 1.702
LN_EPS = 1e-5
RMS_EPS = 1e-5
DEEPNORM_ALPHA = (2 * DEPTH) ** 0.25

LANES = 128
QK_FOLD = HEAD_DIM ** -0.5 * math.log2(math.e)
NEG_BIG = -1e30
NT_DIMS = (((1,), (1,)), ((), ()))
VMEM_LIMIT = 56 * 1024 * 1024


def _cparams(*sem):
    return pltpu.CompilerParams(dimension_semantics=sem, vmem_limit_bytes=VMEM_LIMIT)


def _pick(n, pref):
    t = min(pref, n)
    while n % t:
        t //= 2
    return t


def _qkv_kernel(x_ref, w_ref, c_ref, s_ref, o_ref, *, n_q_tiles, n_rope_tiles):
    j = pl.program_id(0)
    acc = jnp.dot(x_ref[...], w_ref[...], preferred_element_type=jnp.float32)
    tn = acc.shape[1]

    @pl.when(j < n_rope_tiles)
    def _():
        fold = jnp.where(j < n_q_tiles, QK_FOLD, 1.0)
        c = c_ref[...] * fold
        s = s_ref[...] * fold
        lane = lax.broadcasted_iota(jnp.int32, c.shape, 1)
        for h in range(tn // LANES):
            a = acc[:, h * LANES:(h + 1) * LANES]
            partner = jnp.where(lane < ROT_HALF,
                                pltpu.roll(a, LANES - ROT_HALF, 1),
                                pltpu.roll(a, ROT_HALF, 1))
            o_ref[:, h * LANES:(h + 1) * LANES] = (a * c + partner * s).astype(o_ref.dtype)

    @pl.when(j >= n_rope_tiles)
    def _():
        o_ref[...] = acc.astype(o_ref.dtype)


def qkv_proj(x_bf, w_bf, rope_c, rope_s, n_q_cols):
    m, k = x_bf.shape
    n = w_bf.shape[1]
    tm = _pick(m, 512)
    tn = _pick(math.gcd(n, n_q_cols), 1024)
    return pl.pallas_call(
        functools.partial(_qkv_kernel, n_q_tiles=n_q_cols // tn, n_rope_tiles=2 * n_q_cols // tn),
        grid=(n // tn, m // tm),
        in_specs=[
            pl.BlockSpec((tm, k), lambda j, i: (i, 0)),
            pl.BlockSpec((k, tn), lambda j, i: (0, j)),
            pl.BlockSpec((tm, LANES), lambda j, i: (i, 0)),
            pl.BlockSpec((tm, LANES), lambda j, i: (i, 0)),
        ],
        out_specs=pl.BlockSpec((tm, tn), lambda j, i: (i, j)),
        out_shape=jax.ShapeDtypeStruct((m, n), jnp.bfloat16),
        compiler_params=_cparams("parallel", "arbitrary"),
        name="qkv_proj",
    )(x_bf, w_bf, rope_c, rope_s)


def _diff_attn_kernel(lq1_ref, lk1_ref, lq2_ref, lk2_ref, g_ref, q_ref, k_ref, v_ref,
                      o_ref, acc1, acc2, s_a, s_b, *, tq, lambda_init):
    i = pl.program_id(2)
    d = HEAD_DIM
    q = q_ref[0]
    qs = (q[:, :d], q[:, d:])
    accs = (acc1, acc2)

    def qk(j, sbuf):
        kj = k_ref[0, pl.ds(pl.multiple_of(j * tq, tq), tq), :]
        for c in range(2):
            sbuf[c] = lax.dot_general(qs[c], kj[:, c * d:(c + 1) * d], NT_DIMS,
                                      preferred_element_type=jnp.float32)

    def softmax_pv(j, sbuf, carry, masked):
        vj = v_ref[0, pl.ds(pl.multiple_of(j * tq, tq), tq), :]
        new = []
        for c in range(2):
            m_old, l_old = carry[2 * c], carry[2 * c + 1]
            s = sbuf[c]
            if masked:
                row = lax.broadcasted_iota(jnp.int32, s.shape, 0)
                col = lax.broadcasted_iota(jnp.int32, s.shape, 1)
                s = jnp.where(col <= row, s, NEG_BIG)
            m_new = jnp.maximum(m_old, jnp.max(s, axis=1, keepdims=True))
            alpha = jnp.exp2(m_old - m_new)
            p = jnp.exp2(s - m_new)
            l_new = alpha * l_old + jnp.sum(p, axis=1, keepdims=True)
            accs[c][...] = alpha * accs[c][...] + jnp.dot(
                p.astype(vj.dtype), vj, preferred_element_type=jnp.float32)
            new += [m_new, l_new]
        return tuple(new)

    acc1[...] = jnp.zeros_like(acc1)
    acc2[...] = jnp.zeros_like(acc2)
    m0 = jnp.full((tq, 1), NEG_BIG, jnp.float32)
    l0 = jnp.zeros((tq, 1), jnp.float32)

    qk(0, s_a)

    def pair(t, carry):
        j = 2 * t
        qk(j + 1, s_b)
        carry = softmax_pv(j, s_a, carry, False)
        qk(j + 2, s_a)
        return softmax_pv(j + 1, s_b, carry, False)

    carry = lax.fori_loop(0, i // 2, pair, (m0, l0, m0, l0))

    def last_even(carry):
        return softmax_pv(i, s_a, carry, True)

    def last_odd(carry):
        qk(i, s_b)
        carry = softmax_pv(i - 1, s_a, carry, False)
        return softmax_pv(i, s_b, carry, True)

    m1, l1, m2, l2 = lax.cond(i % 2 == 0, last_even, last_odd, carry)

    lam = (jnp.exp(jnp.sum(lq1_ref[...] * lk1_ref[...], axis=1, keepdims=True))
           - jnp.exp(jnp.sum(lq2_ref[...] * lk2_ref[...], axis=1, keepdims=True))
           + lambda_init)
    o = acc1[...] / l1 - lam * (acc2[...] / l2)
    o = o * lax.rsqrt(jnp.mean(o * o, axis=1, keepdims=True) + RMS_EPS)
    o = o * g_ref[...] * (1.0 - lambda_init)
    o_ref[0] = o.astype(o_ref.dtype)


def diff_attention(qkv, lq1, lk1, lq2, lk2, subln_g, n_heads, lambda_init):
    b, s, _ = qkv.shape
    dv = DIFF_V_DIM
    tq = _pick(s, 512)
    vec = lambda a: a.reshape(1, -1).astype(jnp.float32)
    small = lambda n: pl.BlockSpec((1, n), lambda bi, h, i: (0, 0))
    return pl.pallas_call(
        functools.partial(_diff_attn_kernel, tq=tq, lambda_init=lambda_init),
        grid=(b, n_heads, s // tq),
        in_specs=[
            small(HEAD_DIM), small(HEAD_DIM), small(HEAD_DIM), small(HEAD_DIM), small(dv),
            pl.BlockSpec((1, tq, dv), lambda bi, h, i: (bi, i, h)),
            pl.BlockSpec((1, s, dv), lambda bi, h, i: (bi, 0, n_heads + h)),
            pl.BlockSpec((1, s, dv), lambda bi, h, i: (bi, 0, 2 * n_heads + h)),
        ],
        out_specs=pl.BlockSpec((1, tq, dv), lambda bi, h, i: (bi, i, h)),
        out_shape=jax.ShapeDtypeStruct((b, s, n_heads * dv), jnp.bfloat16),
        scratch_shapes=[pltpu.VMEM((tq, dv), jnp.float32), pltpu.VMEM((tq, dv), jnp.float32),
                        pltpu.VMEM((2, tq, tq), jnp.float32), pltpu.VMEM((2, tq, tq), jnp.float32)],
        compiler_params=_cparams("parallel", "parallel", "arbitrary"),
        name="diff_attention",
    )(vec(lq1), vec(lk1), vec(lq2), vec(lk2), vec(subln_g), qkv, qkv, qkv)


def _moba_kernel(q_ref, k_ref, v_ref, o_ref, kaug, kmean, acc, s_a, s_b, *, nb, ck):
    i = pl.program_id(2)
    blk = MOBA_BLOCK
    d = HEAD_DIM
    s_len = nb * blk
    n_chunks = s_len // ck
    shift = blk.bit_length() - 1

    @pl.when(i == 0)
    def _():
        r = lax.broadcasted_iota(jnp.int32, (LANES, s_len), 0)
        c = lax.broadcasted_iota(jnp.int32, (LANES, s_len), 1)
        ind = jnp.where((c >> shift) == r, 1.0, 0.0).astype(jnp.bfloat16)
        km = jnp.dot(ind, k_ref[0], preferred_element_type=jnp.float32) * (1.0 / blk)
        hi = km.astype(jnp.bfloat16)
        kmean[0:LANES, :] = hi
        kmean[LANES:2 * LANES, :] = (km - hi.astype(jnp.float32)).astype(jnp.bfloat16)
        rr = lax.broadcasted_iota(jnp.int32, (s_len, LANES), 0)
        cc = lax.broadcasted_iota(jnp.int32, (s_len, LANES), 1)
        kaug[:, 0:d] = k_ref[0]
        kaug[:, d:d + LANES] = jnp.where((rr >> shift) == cc, 1.0, 0.0).astype(jnp.bfloat16)

    q = q_ref[0]
    nbp = -(-nb // 8) * 8
    g2 = lax.dot_general(kmean[...], q, NT_DIMS, preferred_element_type=jnp.float32)
    gate = g2[0:nbp] + g2[LANES:LANES + nbp]
    rowf = lax.broadcasted_iota(jnp.int32, gate.shape, 0).astype(jnp.float32)
    gate = jnp.where(rowf < i.astype(jnp.float32), gate, -jnp.inf)
    sel = jnp.zeros(gate.shape, jnp.float32)
    for _ in range(MOBA_TOPK):
        mx = jnp.max(gate, axis=0, keepdims=True)
        first = jnp.min(jnp.where(gate == mx, rowf, float(LANES)), axis=0, keepdims=True)
        hit = (rowf == first) & (mx > -jnp.inf)
        sel = jnp.where(hit, 1.0, sel)
        gate = jnp.where(rowf == first, -jnp.inf, gate)
    bias_t = jnp.where((sel > 0.0) | (rowf >= float(nb)), 0.0, NEG_BIG)
    if nbp < LANES:
        bias_t = jnp.concatenate([bias_t, jnp.zeros((LANES - nbp, blk), jnp.float32)], axis=0)
    q_aug = jnp.concatenate([q, bias_t.T.astype(jnp.bfloat16)], axis=1)

    def qk(c, sbuf):
        kc = kaug[pl.ds(pl.multiple_of(c * ck, ck), ck), :]
        sbuf[...] = lax.dot_general(q_aug, kc, NT_DIMS, preferred_element_type=jnp.float32)

    def softmax_pv(c, sbuf, carry):
        m_old, l_old = carry
        s = sbuf[...]
        m_new = jnp.maximum(m_old, jnp.max(s, axis=1, keepdims=True))
        alpha = jnp.exp2(m_old - m_new)
        p = jnp.exp2(s - m_new)
        l_new = alpha * l_old + jnp.sum(p, axis=1, keepdims=True)
        vc = v_ref[0, pl.ds(pl.multiple_of(c * ck, ck), ck), :]
        acc[...] = alpha * acc[...] + jnp.dot(p.astype(vc.dtype), vc,
                                              preferred_element_type=jnp.float32)
        return m_new, l_new

    qk(0, s_a)
    own = pl.ds(pl.multiple_of(i * blk, blk), blk)
    s = lax.dot_general(q, k_ref[0, own, :], NT_DIMS, preferred_element_type=jnp.float32)
    row = lax.broadcasted_iota(jnp.int32, s.shape, 0)
    col = lax.broadcasted_iota(jnp.int32, s.shape, 1)
    s = jnp.where(col <= row, s, NEG_BIG)
    m0 = jnp.max(s, axis=1, keepdims=True)
    p = jnp.exp2(s - m0)
    l0 = jnp.sum(p, axis=1, keepdims=True)
    v_own = v_ref[0, own, :]
    acc[...] = jnp.dot(p.astype(v_own.dtype), v_own, preferred_element_type=jnp.float32)

    nc = (i * blk + ck - 1) // ck

    def pair(t, carry):
        c = 2 * t
        qk(c + 1, s_b)
        carry = softmax_pv(c, s_a, carry)
        qk(jnp.minimum(c + 2, n_chunks - 1), s_a)
        return softmax_pv(c + 1, s_b, carry)

    carry = lax.fori_loop(0, nc // 2, pair, (m0, l0))
    _, l_fin = lax.cond(nc % 2 == 1, lambda cr: softmax_pv(nc - 1, s_a, cr), lambda cr: cr, carry)
    o_ref[0] = (acc[...] / l_fin).astype(o_ref.dtype)


def moba_attention(qkv, n_heads):
    b, s, _ = qkv.shape
    d = HEAD_DIM
    blk = MOBA_BLOCK
    assert s % blk == 0 and blk & (blk - 1) == 0
    nb = s // blk
    assert nb <= LANES
    ck = _pick(s, 4 * blk)
    return pl.pallas_call(
        functools.partial(_moba_kernel, nb=nb, ck=ck),
        grid=(b, n_heads, nb),
        in_specs=[
            pl.BlockSpec((1, blk, d), lambda bi, h, i: (bi, i, h)),
            pl.BlockSpec((1, s, d), lambda bi, h, i: (bi, 0, n_heads + h)),
            pl.BlockSpec((1, s, d), lambda bi, h, i: (bi, 0, 2 * n_heads + h)),
        ],
        out_specs=pl.BlockSpec((1, blk, d), lambda bi, h, i: (bi, i, h)),
        out_shape=jax.ShapeDtypeStruct((b, s, n_heads * d), jnp.bfloat16),
        scratch_shapes=[pltpu.VMEM((s, d + LANES), jnp.bfloat16),
                        pltpu.VMEM((2 * LANES, d), jnp.bfloat16),
                        pltpu.VMEM((blk, d), jnp.float32),
                        pltpu.VMEM((blk, ck), jnp.float32),
                        pltpu.VMEM((blk, ck), jnp.float32)],
        compiler_params=_cparams("parallel", "parallel", "arbitrary"),
        name="moba_attention",
    )(qkv, qkv, qkv)


def _proj_resid_kernel(a_ref, w_ref, r_ref, o_ref):
    o_ref[...] = DEEPNORM_ALPHA * r_ref[...] + jnp.dot(
        a_ref[...], w_ref[...], preferred_element_type=jnp.float32)


def proj_resid(a_bf, w_bf, resid):
    m, k = a_bf.shape
    n = w_bf.shape[1]
    tm = _pick(m, 512)
    tn = _pick(n, 1024)
    return pl.pallas_call(
        _proj_resid_kernel,
        grid=(n // tn, m // tm),
        in_specs=[
            pl.BlockSpec((tm, k), lambda j, i: (i, 0)),
            pl.BlockSpec((k, tn), lambda j, i: (0, j)),
            pl.BlockSpec((tm, tn), lambda j, i: (i, j)),
        ],
        out_specs=pl.BlockSpec((tm, tn), lambda j, i: (i, j)),
        out_shape=jax.ShapeDtypeStruct((m, n), jnp.float32),
        compiler_params=_cparams("parallel", "arbitrary"),
        name="proj_resid",
    )(a_bf, w_bf, resid)


def _layer_norm_rows(z, g, b):
    mu = jnp.mean(z, axis=1, keepdims=True)
    zc = z - mu
    var = jnp.mean(zc * zc, axis=1, keepdims=True)
    return zc * lax.rsqrt(var + LN_EPS) * g + b


def _ln_router_kernel(z_ref, g_ref, b_ref, rw_ref, rb_ref,
                      y_ref, idx_ref, wt_ref, rank_ref, cnt_ref, carry):
    t = pl.program_id(0)

    @pl.when(t == 0)
    def _():
        carry[...] = jnp.zeros_like(carry)

    y = _layer_norm_rows(z_ref[...], g_ref[...], b_ref[...])
    y_ref[...] = y
    logits = jnp.dot(y, rw_ref[...], preferred_element_type=jnp.float32,
                     precision=lax.Precision.HIGHEST) + rb_ref[...]
    tm = logits.shape[0]
    lanef = lax.broadcasted_iota(jnp.int32, logits.shape, 1).astype(jnp.float32)
    vals, idxs = [], []
    onehot = jnp.zeros(logits.shape, jnp.float32)
    for _ in range(TOP_K):
        mx = jnp.max(logits, axis=1, keepdims=True)
        first = jnp.min(jnp.where(logits == mx, lanef, float(LANES)), axis=1, keepdims=True)
        hit = lanef == first
        onehot = jnp.where(hit, 1.0, onehot)
        logits = jnp.where(hit, -jnp.inf, logits)
        vals.append(mx)
        idxs.append(first)
    es = [jnp.exp(v - vals[0]) for v in vals]
    den = es[0] + es[1] + es[2] + es[3]

    r = lax.broadcasted_iota(jnp.int32, (tm, tm), 0)
    c = lax.broadcasted_iota(jnp.int32, (tm, tm), 1)
    tri = jnp.where(c < r, 1.0, 0.0).astype(jnp.bfloat16)
    before = jnp.dot(tri, onehot.astype(jnp.bfloat16),
                     preferred_element_type=jnp.float32) + carry[...]
    for k in range(TOP_K):
        rk = jnp.sum(jnp.where(lanef == idxs[k], before, 0.0), axis=1, keepdims=True)
        rank_ref[:, k:k + 1] = rk.astype(jnp.int32)
        idx_ref[:, k:k + 1] = idxs[k].astype(jnp.int32)
        wt_ref[:, k:k + 1] = es[k] / den
    carry[...] = carry[...] + jnp.sum(onehot, axis=0, keepdims=True)
    cnt_ref[...] = carry[...].astype(jnp.int32)


def ln_router(z, g, b, router_w, router_b):
    n, dm = z.shape
    e = router_w.shape[1]
    assert TOP_K <= e <= LANES
    rw = jnp.zeros((dm, LANES), jnp.float32).at[:, :e].set(router_w.astype(jnp.float32))
    rb = jnp.full((1, LANES), NEG_BIG, jnp.float32).at[0, :e].set(router_b.astype(jnp.float32))
    tm = _pick(n, 256)
    row = lambda w: pl.BlockSpec((tm, w), lambda t: (t, 0))
    const = lambda r, w: pl.BlockSpec((r, w), lambda t: (0, 0))
    return pl.pallas_call(
        _ln_router_kernel,
        grid=(n // tm,),
        in_specs=[row(dm), const(1, dm), const(1, dm), const(dm, LANES), const(1, LANES)],
        out_specs=[row(dm), row(TOP_K), row(TOP_K), row(TOP_K), const(1, LANES)],
        out_shape=[
            jax.ShapeDtypeStruct((n, dm), jnp.float32),
            jax.ShapeDtypeStruct((n, TOP_K), jnp.int32),
            jax.ShapeDtypeStruct((n, TOP_K), jnp.float32),
            jax.ShapeDtypeStruct((n, TOP_K), jnp.int32),
            jax.ShapeDtypeStruct((1, LANES), jnp.int32),
        ],
        scratch_shapes=[pltpu.VMEM((1, LANES), jnp.float32)],
        compiler_params=_cparams("arbitrary"),
        name="ln_router",
    )(z, g.reshape(1, dm), b.reshape(1, dm), rw, rb)


def _expert_up_kernel(be_ref, nu_ref, tok_ref, x_hbm, wg_ref, bg_ref, wu_ref, bu_ref, h_ref,
                      xbuf, sem, *, tm):
    t = pl.program_id(0)
    n_used = nu_ref[0]
    slot = t % 2

    def row_copy(blk, r, sl):
        return pltpu.make_async_copy(x_hbm.at[pl.ds(tok_ref[blk * tm + r], 1)],
                                     xbuf.at[sl, pl.ds(r, 1)], sem.at[sl])

    def start_block(blk, sl):
        def body(r, _):
            row_copy(blk, r, sl).start()
            return 0
        lax.fori_loop(0, tm, body, 0, unroll=8)

    def wait_block(blk, sl):
        def body(r, _):
            row_copy(blk, r, sl).wait()
            return 0
        lax.fori_loop(0, tm, body, 0, unroll=8)

    @pl.when((t == 0) & (n_used > 0))
    def _():
        start_block(0, 0)

    @pl.when(t + 1 < n_used)
    def _():
        start_block(t + 1, 1 - slot)

    @pl.when(t < n_used)
    def _():
        wait_block(t, slot)
        x = xbuf[slot].astype(jnp.bfloat16)
        g = jnp.dot(x, wg_ref[0], preferred_element_type=jnp.float32) + bg_ref[0]
        u = jnp.dot(x, wu_ref[0], preferred_element_type=jnp.float32) + bu_ref[0]
        g = jnp.minimum(g, SWIGLU_LIMIT)
        u = jnp.clip(u, -SWIGLU_LIMIT, SWIGLU_LIMIT)
        h = g * jax.nn.sigmoid(SWIGLU_ALPHA * g) * (u + 1.0)
        h_ref[...] = h.astype(h_ref.dtype)

    @pl.when(t >= n_used)
    def _():
        h_ref[...] = jnp.zeros_like(h_ref)


def _expert_down_kernel(be_ref, nu_ref, h_ref, wd_ref, bd_ref, y_ref):
    t = pl.program_id(0)

    @pl.when(t < nu_ref[0])
    def _():
        y_ref[...] = jnp.dot(h_ref[...], wd_ref[0],
                             preferred_element_type=jnp.float32) + bd_ref[0]

    @pl.when(t >= nu_ref[0])
    def _():
        y_ref[...] = jnp.zeros_like(y_ref)


def expert_ffn(x, row_tok, block_exp, n_used, wg, bg, wu, bu, wd, bd, tm):
    p = row_tok.shape[0]
    dm = x.shape[1]
    e, _, f = wg.shape
    nblk = p // tm
    h = pl.pallas_call(
        functools.partial(_expert_up_kernel, tm=tm),
        grid_spec=pltpu.PrefetchScalarGridSpec(
            num_scalar_prefetch=3,
            grid=(nblk,),
            in_specs=[
                pl.BlockSpec(memory_space=pl.ANY),
                pl.BlockSpec((1, dm, f), lambda t, be, nu, tok: (be[t], 0, 0)),
                pl.BlockSpec((1, 1, f), lambda t, be, nu, tok: (be[t], 0, 0)),
                pl.BlockSpec((1, dm, f), lambda t, be, nu, tok: (be[t], 0, 0)),
                pl.BlockSpec((1, 1, f), lambda t, be, nu, tok: (be[t], 0, 0)),
            ],
            out_specs=pl.BlockSpec((tm, f), lambda t, be, nu, tok: (t, 0)),
            scratch_shapes=[pltpu.VMEM((2, tm, dm), x.dtype), pltpu.SemaphoreType.DMA((2,))],
        ),
        out_shape=jax.ShapeDtypeStruct((p, f), jnp.bfloat16),
        compiler_params=_cparams("arbitrary"),
        name="expert_up",
    )(block_exp, n_used, row_tok, x, wg, bg.reshape(e, 1, f), wu, bu.reshape(e, 1, f))
    return pl.pallas_call(
        _expert_down_kernel,
        grid_spec=pltpu.PrefetchScalarGridSpec(
            num_scalar_prefetch=2,
            grid=(nblk,),
            in_specs=[
                pl.BlockSpec((tm, f), lambda t, be, nu: (t, 0)),
                pl.BlockSpec((1, f, dm), lambda t, be, nu: (be[t], 0, 0)),
                pl.BlockSpec((1, 1, dm), lambda t, be, nu: (be[t], 0, 0)),
            ],
            out_specs=pl.BlockSpec((tm, dm), lambda t, be, nu: (t, 0)),
        ),
        out_shape=jax.ShapeDtypeStruct((p, dm), jnp.float32),
        compiler_params=_cparams("arbitrary"),
        name="expert_down",
    )(block_exp, n_used, h, wd, bd.reshape(e, 1, dm))


def _combine_ln_kernel(dest_ref, ys_hbm, wt_ref, x_ref, g_ref, b_ref, o_ref, obf_ref,
                       buf, sem, *, tc):
    t = pl.program_id(0)
    slot = t % 2

    def row_copy(tile, r, k, sl):
        return pltpu.make_async_copy(
            ys_hbm.at[pl.ds(dest_ref[(tile * tc + r) * TOP_K + k], 1)],
            buf.at[sl, k, pl.ds(r, 1)], sem.at[sl])

    def start_tile(tile, sl):
        def body(r, _):
            for k in range(TOP_K):
                row_copy(tile, r, k, sl).start()
            return 0
        lax.fori_loop(0, tc, body, 0, unroll=2)

    def wait_tile(tile, sl):
        def body(r, _):
            for k in range(TOP_K):
                row_copy(tile, r, k, sl).wait()
            return 0
        lax.fori_loop(0, tc, body, 0, unroll=2)

    @pl.when(t == 0)
    def _():
        start_tile(0, 0)

    @pl.when(t + 1 < pl.num_programs(0))
    def _():
        start_tile(t + 1, 1 - slot)

    wait_tile(t, slot)
    z = DEEPNORM_ALPHA * x_ref[...]
    for k in range(TOP_K):
        z = z + wt_ref[:, k:k + 1] * buf[slot, k]
    y = _layer_norm_rows(z, g_ref[...], b_ref[...])
    o_ref[...] = y
    obf_ref[...] = y.astype(obf_ref.dtype)


def combine_ln(ys, dest, wt, x, g, b):
    n, dm = x.shape
    tc = _pick(n, 128)
    row = lambda w: pl.BlockSpec((tc, w), lambda t, d: (t, 0))
    const = lambda w: pl.BlockSpec((1, w), lambda t, d: (0, 0))
    return pl.pallas_call(
        functools.partial(_combine_ln_kernel, tc=tc),
        grid_spec=pltpu.PrefetchScalarGridSpec(
            num_scalar_prefetch=1,
            grid=(n // tc,),
            in_specs=[pl.BlockSpec(memory_space=pl.ANY), row(TOP_K), row(dm), const(dm), const(dm)],
            out_specs=[row(dm), row(dm)],
            scratch_shapes=[pltpu.VMEM((2, TOP_K, tc, dm), jnp.float32),
                            pltpu.SemaphoreType.DMA((2,))],
        ),
        out_shape=[jax.ShapeDtypeStruct((n, dm), jnp.float32),
                   jax.ShapeDtypeStruct((n, dm), jnp.bfloat16)],
        compiler_params=_cparams("arbitrary"),
        name="combine_ln",
    )(dest.reshape(-1), ys, wt, x, g.reshape(1, dm), b.reshape(1, dm))


def _rope_tables(positions):
    inv = 1.0 / (ROPE_THETA ** (jnp.arange(0, ROT_DIM, 2, dtype=jnp.float32) / ROT_DIM))
    ang = positions.astype(jnp.float32).reshape(-1, 1) * inv
    cos, sin = jnp.cos(ang), jnp.sin(ang)
    n = ang.shape[0]
    rest = LANES - ROT_DIM
    c = jnp.concatenate([cos, cos, jnp.ones((n, rest), jnp.float32)], axis=1)
    s = jnp.concatenate([-sin, sin, jnp.zeros((n, rest), jnp.float32)], axis=1)
    return c, s


def moe_ln(z, ln_g, ln_b, router_w, router_b, wg, bg, wu, bu, wd, bd, ln2_g, ln2_b, tm=256):
    n, dm = z.shape
    e = router_w.shape[1]
    x1, idx, wt, rank, cnt = ln_router(z, ln_g, ln_b, router_w, router_b)
    counts = cnt[0, :e]
    padded = (counts + tm - 1) // tm * tm
    pend = jnp.cumsum(padded)
    pstart = pend - padded
    dest = pstart[idx] + rank
    nblk = (n * TOP_K) // tm + e
    tok = jnp.broadcast_to(jnp.arange(n, dtype=jnp.int32)[:, None], (n, TOP_K))
    row_tok = jnp.zeros((nblk * tm,), jnp.int32).at[dest.reshape(-1)].set(tok.reshape(-1))
    starts = jnp.arange(nblk, dtype=jnp.int32) * tm
    block_exp = jnp.minimum(jnp.sum(pend[None, :] <= starts[:, None], axis=1),
                            e - 1).astype(jnp.int32)
    n_used = (pend[-1:] // tm).astype(jnp.int32)
    ys = expert_ffn(x1, row_tok, block_exp, n_used, wg, bg, wu, bu, wd, bd, tm)
    return combine_ln(ys, dest.astype(jnp.int32), wt, x1, ln2_g, ln2_b)


def kernel(x, positions, diff_w_qkv, diff_lambda_q1, diff_lambda_k1, diff_lambda_q2, diff_lambda_k2, diff_subln_g, diff_w_o, moba_w_qkv, moba_w_o, ln_mix_g, ln_mix_b, router_w, router_b, exp_w_gate, exp_b_gate, exp_w_up, exp_b_up, exp_w_down, exp_b_down, ln_ffn_g, ln_ffn_b):
    b, s, dm = x.shape
    n = b * s
    bf = jnp.bfloat16
    rope_c, rope_s = _rope_tables(positions)
    xf = x.reshape(n, dm)
    x_bf = xf.astype(bf)
    n_mixers = 2
    for i in range(DEPTH):
        j = i // n_mixers
        if i % n_mixers == 0:
            heads = dm // DIFF_V_DIM
            lambda_init = 0.8 - 0.6 * math.exp(-0.3 * i)
            qkv = qkv_proj(x_bf, diff_w_qkv[j].astype(bf), rope_c, rope_s, heads * DIFF_V_DIM)
            att = diff_attention(qkv.reshape(b, s, -1), diff_lambda_q1[j], diff_lambda_k1[j],
                                 diff_lambda_q2[j], diff_lambda_k2[j], diff_subln_g[j],
                                 heads, lambda_init)
            w_o = diff_w_o[j]
        else:
            heads = dm // HEAD_DIM
            qkv = qkv_proj(x_bf, moba_w_qkv[j].astype(bf), rope_c, rope_s, heads * HEAD_DIM)
            att = moba_attention(qkv.reshape(b, s, -1), heads)
            w_o = moba_w_o[j]
        z = proj_resid(att.reshape(n, -1), w_o.astype(bf), xf)
        xf, x_bf = moe_ln(z, ln_mix_g[i], ln_mix_b[i], router_w[i], router_b[i],
                          exp_w_gate[i].astype(bf), exp_b_gate[i], exp_w_up[i].astype(bf),
                          exp_b_up[i], exp_w_down[i].astype(bf), exp_b_down[i],
                          ln_ffn_g[i], ln_ffn_b[i])
    return xf.reshape(b, s, dm)
```

```python
import functools
import math

import jax
import jax.numpy as jnp
from jax import lax
from jax.experimental import pallas as pl
from jax.experimental.pallas import tpu as pltpu

DEPTH = 2
ROPE_THETA = 500000.0
HEAD_DIM = 128
ROT_DIM = HEAD_DIM // 4
ROT_HALF = ROT_DIM // 2
DIFF_V_DIM = 2 * HEAD_DIM
MOBA_BLOCK = 256
MOBA_TOPK = 3
TOP_K = 4
SWIGLU_LIMIT = 7.0
SWIGLU_ALPHA = 1.702
LN_EPS = 1e-5
RMS_EPS = 1e-5
DEEPNORM_ALPHA = (2 * DEPTH) ** 0.25

LANES = 128
QK_FOLD = HEAD_DIM ** -0.5 * math.log2(math.e)
NEG_BIG = -1e30
NT_DIMS = (((1,), (1,)), ((), ()))
VMEM_LIMIT = 56 * 1024 * 1024


def _cparams(*sem):
    return pltpu.CompilerParams(dimension_semantics=sem, vmem_limit_bytes=VMEM_LIMIT)


def _pick(n, pref):
    t = min(pref, n)
    while n % t:
        t //= 2
    return t


def _cast_weight_tile(w_ref, wbf):
    @pl.when(pl.program_id(1) == 0)
    def _():
        wbf[...] = w_ref[0].astype(wbf.dtype)


def _qkv_kernel(x_ref, w_ref, c_ref, s_ref, o_ref, wbf, *, n_q_tiles, n_rope_tiles):
    j = pl.program_id(0)
    _cast_weight_tile(w_ref, wbf)
    acc = jnp.dot(x_ref[...], wbf[...], preferred_element_type=jnp.float32)
    tn = acc.shape[1]

    @pl.when(j < n_rope_tiles)
    def _():
        fold = jnp.where(j < n_q_tiles, QK_FOLD, 1.0)
        c = c_ref[...] * fold
        s = s_ref[...] * fold
        lane = lax.broadcasted_iota(jnp.int32, c.shape, 1)
        for h in range(tn // LANES):
            a = acc[:, h * LANES:(h + 1) * LANES]
            partner = jnp.where(lane < ROT_HALF,
                                pltpu.roll(a, LANES - ROT_HALF, 1),
                                pltpu.roll(a, ROT_HALF, 1))
            o_ref[:, h * LANES:(h + 1) * LANES] = (a * c + partner * s).astype(o_ref.dtype)

    @pl.when(j >= n_rope_tiles)
    def _():
        o_ref[...] = acc.astype(o_ref.dtype)


def qkv_proj(x_bf, w, layer, rope_c, rope_s, n_q_cols):
    m, k = x_bf.shape
    n = w.shape[2]
    tm = _pick(m, 1024)
    tn = _pick(math.gcd(n, n_q_cols), 512)
    return pl.pallas_call(
        functools.partial(_qkv_kernel, n_q_tiles=n_q_cols // tn, n_rope_tiles=2 * n_q_cols // tn),
        grid=(n // tn, m // tm),
        in_specs=[
            pl.BlockSpec((tm, k), lambda j, i: (i, 0)),
            pl.BlockSpec((1, k, tn), lambda j, i: (layer, 0, j)),
            pl.BlockSpec((tm, LANES), lambda j, i: (i, 0)),
            pl.BlockSpec((tm, LANES), lambda j, i: (i, 0)),
        ],
        out_specs=pl.BlockSpec((tm, tn), lambda j, i: (i, j)),
        out_shape=jax.ShapeDtypeStruct((m, n), jnp.bfloat16),
        scratch_shapes=[pltpu.VMEM((k, tn), jnp.bfloat16)],
        compiler_params=_cparams("parallel", "arbitrary"),
        name="qkv_proj",
    )(x_bf, w, rope_c, rope_s)


def _diff_attn_kernel(lq1_ref, lk1_ref, lq2_ref, lk2_ref, g_ref, q_ref, k_ref, v_ref,
                      o_ref, acc1, acc2, s_a, s_b, *, tq, lambda_init):
    i = pl.program_id(2)
    d = HEAD_DIM
    q = q_ref[0]
    qs = (q[:, :d], q[:, d:])
    accs = (acc1, acc2)

    def qk(j, sbuf):
        kj = k_ref[0, pl.ds(pl.multiple_of(j * tq, tq), tq), :]
        for c in range(2):
            sbuf[c] = lax.dot_general(qs[c], kj[:, c * d:(c + 1) * d], NT_DIMS,
                                      preferred_element_type=jnp.float32)

    def softmax_pv(j, sbuf, carry, masked):
        vj = v_ref[0, pl.ds(pl.multiple_of(j * tq, tq), tq), :]
        new = []
        for c in range(2):
            m_old, l_old = carry[2 * c], carry[2 * c + 1]
            s = sbuf[c]
            if masked:
                row = lax.broadcasted_iota(jnp.int32, s.shape, 0)
                col = lax.broadcasted_iota(jnp.int32, s.shape, 1)
                s = jnp.where(col <= row, s, NEG_BIG)
            m_new = jnp.maximum(m_old, jnp.max(s, axis=1, keepdims=True))
            alpha = jnp.exp2(m_old - m_new)
            p = jnp.exp2(s - m_new)
            l_new = alpha * l_old + jnp.sum(p, axis=1, keepdims=True)
            accs[c][...] = alpha * accs[c][...] + jnp.dot(
                p.astype(vj.dtype), vj, preferred_element_type=jnp.float32)
            new += [m_new, l_new]
        return tuple(new)

    acc1[...] = jnp.zeros_like(acc1)
    acc2[...] = jnp.zeros_like(acc2)
    m0 = jnp.full((tq, 1), NEG_BIG, jnp.float32)
    l0 = jnp.zeros((tq, 1), jnp.float32)

    qk(0, s_a)

    def pair(t, carry):
        j = 2 * t
        qk(j + 1, s_b)
        carry = softmax_pv(j, s_a, carry, False)
        qk(j + 2, s_a)
        return softmax_pv(j + 1, s_b, carry, False)

    carry = lax.fori_loop(0, i // 2, pair, (m0, l0, m0, l0))

    def last_even(carry):
        return softmax_pv(i, s_a, carry, True)

    def last_odd(carry):
        qk(i, s_b)
        carry = softmax_pv(i - 1, s_a, carry, False)
        return softmax_pv(i, s_b, carry, True)

    m1, l1, m2, l2 = lax.cond(i % 2 == 0, last_even, last_odd, carry)

    lam = (jnp.exp(jnp.sum(lq1_ref[...] * lk1_ref[...], axis=1, keepdims=True))
           - jnp.exp(jnp.sum(lq2_ref[...] * lk2_ref[...], axis=1, keepdims=True))
           + lambda_init)
    o = acc1[...] / l1 - lam * (acc2[...] / l2)
    o = o * lax.rsqrt(jnp.mean(o * o, axis=1, keepdims=True) + RMS_EPS)
    o = o * g_ref[...] * (1.0 - lambda_init)
    o_ref[0] = o.astype(o_ref.dtype)


def diff_attention(qkv, lq1, lk1, lq2, lk2, subln_g, n_heads, lambda_init):
    b, s, _ = qkv.shape
    dv = DIFF_V_DIM
    tq = _pick(s, 512)
    vec = lambda a: a.reshape(1, -1).astype(jnp.float32)
    small = lambda n: pl.BlockSpec((1, n), lambda bi, h, i: (0, 0))
    return pl.pallas_call(
        functools.partial(_diff_attn_kernel, tq=tq, lambda_init=lambda_init),
        grid=(b, n_heads, s // tq),
        in_specs=[
            small(HEAD_DIM), small(HEAD_DIM), small(HEAD_DIM), small(HEAD_DIM), small(dv),
            pl.BlockSpec((1, tq, dv), lambda bi, h, i: (bi, i, h)),
            pl.BlockSpec((1, s, dv), lambda bi, h, i: (bi, 0, n_heads + h)),
            pl.BlockSpec((1, s, dv), lambda bi, h, i: (bi, 0, 2 * n_heads + h)),
        ],
        out_specs=pl.BlockSpec((1, tq, dv), lambda bi, h, i: (bi, i, h)),
        out_shape=jax.ShapeDtypeStruct((b, s, n_heads * dv), jnp.bfloat16),
        scratch_shapes=[pltpu.VMEM((tq, dv), jnp.float32), pltpu.VMEM((tq, dv), jnp.float32),
                        pltpu.VMEM((2, tq, tq), jnp.float32), pltpu.VMEM((2, tq, tq), jnp.float32)],
        compiler_params=_cparams("parallel", "parallel", "arbitrary"),
        name="diff_attention",
    )(vec(lq1), vec(lk1), vec(lq2), vec(lk2), vec(subln_g), qkv, qkv, qkv)


def _moba_kernel(q_ref, k_ref, v_ref, o_ref, kaug, vaug, kmean, cmask, acc, s_a, s_b,
                 *, nb, ck, qb):
    t = pl.program_id(2)
    blk = MOBA_BLOCK
    d = HEAD_DIM
    tq = qb * blk
    s_len = nb * blk
    n_chunks = s_len // ck
    shift = blk.bit_length() - 1

    @pl.when(t == 0)
    def _():
        vaug[:, 0:d] = v_ref[0]
        vaug[:, d:2 * d] = jnp.ones((s_len, d), vaug.dtype)
        row = lax.broadcasted_iota(jnp.int32, (tq, ck), 0)
        col = lax.broadcasted_iota(jnp.int32, (tq, ck), 1)
        ahead = (col & (blk - 1)) > (row & (blk - 1))
        for pos in range(ck // tq):
            same_blk = (col >> shift) == (pos * qb + (row >> shift))
            cmask[pos] = jnp.where(same_blk & ahead, NEG_BIG, 0.0)
        r = lax.broadcasted_iota(jnp.int32, (LANES, s_len), 0)
        c = lax.broadcasted_iota(jnp.int32, (LANES, s_len), 1)
        ind = jnp.where((c >> shift) == r, 1.0, 0.0).astype(jnp.bfloat16)
        km = jnp.dot(ind, k_ref[0], preferred_element_type=jnp.float32) * (1.0 / blk)
        hi = km.astype(jnp.bfloat16)
        kmean[0:LANES, :] = hi
        kmean[LANES:2 * LANES, :] = (km - hi.astype(jnp.float32)).astype(jnp.bfloat16)
        rr = lax.broadcasted_iota(jnp.int32, (s_len, LANES), 0)
        cc = lax.broadcasted_iota(jnp.int32, (s_len, LANES), 1)
        kaug[:, 0:d] = k_ref[0]
        kaug[:, d:d + LANES] = jnp.where((rr >> shift) == cc, 1.0, 0.0).astype(jnp.bfloat16)

    q = q_ref[0]
    nbp = -(-nb // 8) * 8
    g2 = lax.dot_general(kmean[...], q, NT_DIMS, preferred_element_type=jnp.float32)
    gate = g2[0:nbp] + g2[LANES:LANES + nbp]
    rowf = lax.broadcasted_iota(jnp.int32, gate.shape, 0).astype(jnp.float32)
    own_blk = (t * qb + (lax.broadcasted_iota(jnp.int32, (1, tq), 1) >> shift)).astype(jnp.float32)
    gate = jnp.where(rowf < own_blk, gate, -jnp.inf)
    sel = jnp.zeros(gate.shape, jnp.float32)
    for _ in range(MOBA_TOPK):
        mx = jnp.max(gate, axis=0, keepdims=True)
        first = jnp.min(jnp.where(gate == mx, rowf, float(LANES)), axis=0, keepdims=True)
        hit = (rowf == first) & (mx > -jnp.inf)
        sel = jnp.where(hit, 1.0, sel)
        gate = jnp.where(rowf == first, -jnp.inf, gate)
    bias_t = jnp.where((sel > 0.0) | (rowf == own_blk) | (rowf >= float(nb)), 0.0, NEG_BIG)
    if nbp < LANES:
        bias_t = jnp.concatenate([bias_t, jnp.zeros((LANES - nbp, tq), jnp.float32)], axis=0)
    q_aug = jnp.concatenate([q, bias_t.T.astype(jnp.bfloat16)], axis=1)

    def qk(c, sbuf):
        kc = kaug[pl.ds(pl.multiple_of(c * ck, ck), ck), :]
        sbuf[...] = lax.dot_general(q_aug, kc, NT_DIMS, preferred_element_type=jnp.float32)

    def softmax_pv(c, sbuf, m_old, causal):
        s = sbuf[...]
        if causal:
            s = s + cmask[t % (ck // tq)]
        m_new = jnp.maximum(m_old, jnp.max(s, axis=1, keepdims=True))
        p = jnp.exp2(s - m_new).astype(vaug.dtype)
        vc = vaug[pl.ds(pl.multiple_of(c * ck, ck), ck), :]
        acc[...] = jnp.exp2(m_old - m_new) * acc[...] + jnp.dot(
            p, vc, preferred_element_type=jnp.float32)
        return m_new

    nc = (t * tq + tq + ck - 1) // ck
    acc[...] = jnp.zeros_like(acc)
    qk(nc - 1, s_a)
    qk(0, s_b)
    m_run = softmax_pv(nc - 1, s_a, jnp.full((tq, 1), NEG_BIG, jnp.float32), True)

    def pair(u, m_run):
        c = 2 * u
        qk(c + 1, s_a)
        m_run = softmax_pv(c, s_b, m_run, False)
        qk(jnp.minimum(c + 2, n_chunks - 1), s_b)
        return softmax_pv(c + 1, s_a, m_run, False)

    m_run = lax.fori_loop(0, (nc - 1) // 2, pair, m_run)

    @pl.when((nc - 1) % 2 == 1)
    def _():
        softmax_pv(nc - 2, s_b, m_run, False)

    o_ref[0] = (acc[:, 0:d] / acc[:, d:2 * d]).astype(o_ref.dtype)


def moba_attention(qkv, n_heads):
    b, s, _ = qkv.shape
    d = HEAD_DIM
    blk = MOBA_BLOCK
    assert s % blk == 0 and blk & (blk - 1) == 0
    nb = s // blk
    assert nb <= LANES
    ck = _pick(s, 4 * blk)
    qb = math.gcd(nb, ck // blk)
    tq = qb * blk
    return pl.pallas_call(
        functools.partial(_moba_kernel, nb=nb, ck=ck, qb=qb),
        grid=(b, n_heads, nb // qb),
        in_specs=[
            pl.BlockSpec((1, tq, d), lambda bi, h, i: (bi, i, h)),
            pl.BlockSpec((1, s, d), lambda bi, h, i: (bi, 0, n_heads + h)),
            pl.BlockSpec((1, s, d), lambda bi, h, i: (bi, 0, 2 * n_heads + h)),
        ],
        out_specs=pl.BlockSpec((1, tq, d), lambda bi, h, i: (bi, i, h)),
        out_shape=jax.ShapeDtypeStruct((b, s, n_heads * d), jnp.bfloat16),
        scratch_shapes=[pltpu.VMEM((s, d + LANES), jnp.bfloat16),
                        pltpu.VMEM((s, 2 * d), jnp.bfloat16),
                        pltpu.VMEM((2 * LANES, d), jnp.bfloat16),
                        pltpu.VMEM((ck // tq, tq, ck), jnp.float32),
                        pltpu.VMEM((tq, 2 * d), jnp.float32),
                        pltpu.VMEM((tq, ck), jnp.float32),
                        pltpu.VMEM((tq, ck), jnp.float32)],
        compiler_params=_cparams("parallel", "parallel", "arbitrary"),
        name="moba_attention",
    )(qkv, qkv, qkv)


def _proj_resid_kernel(a_ref, w_ref, r_ref, o_ref, wbf):
    _cast_weight_tile(w_ref, wbf)
    o_ref[...] = DEEPNORM_ALPHA * r_ref[...] + jnp.dot(
        a_ref[...], wbf[...], preferred_element_type=jnp.float32)


def proj_resid(a_bf, w, layer, resid):
    m, k = a_bf.shape
    n = w.shape[2]
    tm = _pick(m, 1024)
    tn = _pick(n, 512)
    return pl.pallas_call(
        _proj_resid_kernel,
        grid=(n // tn, m // tm),
        in_specs=[
            pl.BlockSpec((tm, k), lambda j, i: (i, 0)),
            pl.BlockSpec((1, k, tn), lambda j, i: (layer, 0, j)),
            pl.BlockSpec((tm, tn), lambda j, i: (i, j)),
        ],
        out_specs=pl.BlockSpec((tm, tn), lambda j, i: (i, j)),
        out_shape=jax.ShapeDtypeStruct((m, n), jnp.float32),
        scratch_shapes=[pltpu.VMEM((k, tn), jnp.bfloat16)],
        compiler_params=_cparams("parallel", "arbitrary"),
        name="proj_resid",
    )(a_bf, w, resid)


def _layer_norm_rows(z, g, b):
    mu = jnp.mean(z, axis=1, keepdims=True)
    zc = z - mu
    var = jnp.mean(zc * zc, axis=1, keepdims=True)
    return zc * lax.rsqrt(var + LN_EPS) * g + b


def _ln_router_kernel(z_ref, g_ref, b_ref, rw_ref, rb_ref,
                      y_ref, idx_ref, wt_ref, rank_ref, cnt_ref, carry):
    t = pl.program_id(0)

    @pl.when(t == 0)
    def _():
        carry[...] = jnp.zeros_like(carry)

    y = _layer_norm_rows(z_ref[...], g_ref[...], b_ref[...])
    y_ref[...] = y
    logits = jnp.dot(y, rw_ref[...], preferred_element_type=jnp.float32,
                     precision=lax.Precision.HIGHEST) + rb_ref[...]
    tm = logits.shape[0]
    lanef = lax.broadcasted_iota(jnp.int32, logits.shape, 1).astype(jnp.float32)
    vals, idxs = [], []
    onehot = jnp.zeros(logits.shape, jnp.float32)
    for _ in range(TOP_K):
        mx = jnp.max(logits, axis=1, keepdims=True)
        first = jnp.min(jnp.where(logits == mx, lanef, float(LANES)), axis=1, keepdims=True)
        hit = lanef == first
        onehot = jnp.where(hit, 1.0, onehot)
        logits = jnp.where(hit, -jnp.inf, logits)
        vals.append(mx)
        idxs.append(first)
    es = [jnp.exp(v - vals[0]) for v in vals]
    den = es[0] + es[1] + es[2] + es[3]

    r = lax.broadcasted_iota(jnp.int32, (tm, tm), 0)
    c = lax.broadcasted_iota(jnp.int32, (tm, tm), 1)
    tri = jnp.where(c < r, 1.0, 0.0).astype(jnp.bfloat16)
    before = jnp.dot(tri, onehot.astype(jnp.bfloat16),
                     preferred_element_type=jnp.float32) + carry[...]
    for k in range(TOP_K):
        rk = jnp.sum(jnp.where(lanef == idxs[k], before, 0.0), axis=1, keepdims=True)
        rank_ref[:, k:k + 1] = rk.astype(jnp.int32)
        idx_ref[:, k:k + 1] = idxs[k].astype(jnp.int32)
        wt_ref[:, k:k + 1] = es[k] / den
    carry[...] = carry[...] + jnp.sum(onehot, axis=0, keepdims=True)
    cnt_ref[...] = carry[...].astype(jnp.int32)


def ln_router(z, g, b, router_w, router_b):
    n, dm = z.shape
    e = router_w.shape[1]
    assert TOP_K <= e <= LANES
    rw = jnp.zeros((dm, LANES), jnp.float32).at[:, :e].set(router_w.astype(jnp.float32))
    rb = jnp.full((1, LANES), NEG_BIG, jnp.float32).at[0, :e].set(router_b.astype(jnp.float32))
    tm = _pick(n, 256)
    row = lambda w: pl.BlockSpec((tm, w), lambda t: (t, 0))
    const = lambda r, w: pl.BlockSpec((r, w), lambda t: (0, 0))
    return pl.pallas_call(
        _ln_router_kernel,
        grid=(n // tm,),
        in_specs=[row(dm), const(1, dm), const(1, dm), const(dm, LANES), const(1, LANES)],
        out_specs=[row(dm), row(TOP_K), row(TOP_K), row(TOP_K), const(1, LANES)],
        out_shape=[
            jax.ShapeDtypeStruct((n, dm), jnp.float32),
            jax.ShapeDtypeStruct((n, TOP_K), jnp.int32),
            jax.ShapeDtypeStruct((n, TOP_K), jnp.float32),
            jax.ShapeDtypeStruct((n, TOP_K), jnp.int32),
            jax.ShapeDtypeStruct((1, LANES), jnp.int32),
        ],
        scratch_shapes=[pltpu.VMEM((1, LANES), jnp.float32)],
        compiler_params=_cparams("arbitrary"),
        name="ln_router",
    )(z, g.reshape(1, dm), b.reshape(1, dm), rw, rb)


def _expert_up_kernel(be_ref, nu_ref, tok_ref, x_hbm, wg_ref, bg_ref, wu_ref, bu_ref, h_ref,
                      xbuf, sem, *, tm):
    t = pl.program_id(0)
    n_used = nu_ref[0]
    slot = t % 2

    def row_copy(blk, r, sl):
        return pltpu.make_async_copy(x_hbm.at[pl.ds(tok_ref[blk * tm + r], 1)],
                                     xbuf.at[sl, pl.ds(r, 1)], sem.at[sl])

    def start_block(blk, sl):
        def body(r, _):
            row_copy(blk, r, sl).start()
            return 0
        lax.fori_loop(0, tm, body, 0, unroll=8)

    def wait_block(blk, sl):
        def body(r, _):
            row_copy(blk, r, sl).wait()
            return 0
        lax.fori_loop(0, tm, body, 0, unroll=8)

    @pl.when((t == 0) & (n_used > 0))
    def _():
        start_block(0, 0)

    @pl.when(t + 1 < n_used)
    def _():
        start_block(t + 1, 1 - slot)

    @pl.when(t < n_used)
    def _():
        wait_block(t, slot)
        x = xbuf[slot].astype(jnp.bfloat16)
        g = jnp.dot(x, wg_ref[0], preferred_element_type=jnp.float32) + bg_ref[0]
        u = jnp.dot(x, wu_ref[0], preferred_element_type=jnp.float32) + bu_ref[0]
        g = jnp.minimum(g, SWIGLU_LIMIT)
        u = jnp.clip(u, -SWIGLU_LIMIT, SWIGLU_LIMIT)
        h = g * jax.nn.sigmoid(SWIGLU_ALPHA * g) * (u + 1.0)
        h_ref[...] = h.astype(h_ref.dtype)

    @pl.when(t >= n_used)
    def _():
        h_ref[...] = jnp.zeros_like(h_ref)


def _expert_down_kernel(be_ref, nu_ref, h_ref, wd_ref, bd_ref, y_ref, wbf):
    t = pl.program_id(0)

    @pl.when((t == 0) | (be_ref[t] != be_ref[jnp.maximum(t - 1, 0)]))
    def _():
        wbf[...] = wd_ref[0, 0].astype(wbf.dtype)

    @pl.when(t < nu_ref[0])
    def _():
        y_ref[...] = jnp.dot(h_ref[...], wbf[...],
                             preferred_element_type=jnp.float32) + bd_ref[0]

    @pl.when(t >= nu_ref[0])
    def _():
        y_ref[...] = jnp.zeros_like(y_ref)


def expert_ffn(x, row_tok, block_exp, n_used, wg, bg, wu, bu, wd, layer, bd, tm):
    p = row_tok.shape[0]
    dm = x.shape[1]
    e, _, f = wg.shape
    nblk = p // tm
    h = pl.pallas_call(
        functools.partial(_expert_up_kernel, tm=tm),
        grid_spec=pltpu.PrefetchScalarGridSpec(
            num_scalar_prefetch=3,
            grid=(nblk,),
            in_specs=[
                pl.BlockSpec(memory_space=pl.ANY),
                pl.BlockSpec((1, dm, f), lambda t, be, nu, tok: (be[t], 0, 0)),
                pl.BlockSpec((1, 1, f), lambda t, be, nu, tok: (be[t], 0, 0)),
                pl.BlockSpec((1, dm, f), lambda t, be, nu, tok: (be[t], 0, 0)),
                pl.BlockSpec((1, 1, f), lambda t, be, nu, tok: (be[t], 0, 0)),
            ],
            out_specs=pl.BlockSpec((tm, f), lambda t, be, nu, tok: (t, 0)),
            scratch_shapes=[pltpu.VMEM((2, tm, dm), x.dtype), pltpu.SemaphoreType.DMA((2,))],
        ),
        out_shape=jax.ShapeDtypeStruct((p, f), jnp.bfloat16),
        compiler_params=_cparams("arbitrary"),
        name="expert_up",
    )(block_exp, n_used, row_tok, x, wg, bg.reshape(e, 1, f), wu, bu.reshape(e, 1, f))
    return pl.pallas_call(
        _expert_down_kernel,
        grid_spec=pltpu.PrefetchScalarGridSpec(
            num_scalar_prefetch=2,
            grid=(nblk,),
            in_specs=[
                pl.BlockSpec((tm, f), lambda t, be, nu: (t, 0)),
                pl.BlockSpec((1, 1, f, dm), lambda t, be, nu: (layer, be[t], 0, 0)),
                pl.BlockSpec((1, 1, dm), lambda t, be, nu: (be[t], 0, 0)),
            ],
            out_specs=pl.BlockSpec((tm, dm), lambda t, be, nu: (t, 0)),
            scratch_shapes=[pltpu.VMEM((f, dm), jnp.bfloat16)],
        ),
        out_shape=jax.ShapeDtypeStruct((p, dm), jnp.float32),
        compiler_params=_cparams("arbitrary"),
        name="expert_down",
    )(block_exp, n_used, h, wd, bd.reshape(e, 1, dm))


def _combine_ln_kernel(dest_ref, ys_hbm, wt_ref, x_ref, g_ref, b_ref, o_ref, obf_ref,
                       buf, sem, *, tc):
    t = pl.program_id(0)
    slot = t % 2

    def row_copy(tile, r, k, sl):
        return pltpu.make_async_copy(
            ys_hbm.at[pl.ds(dest_ref[(tile * tc + r) * TOP_K + k], 1)],
            buf.at[sl, k, pl.ds(r, 1)], sem.at[sl])

    def start_tile(tile, sl):
        def body(r, _):
            for k in range(TOP_K):
                row_copy(tile, r, k, sl).start()
            return 0
        lax.fori_loop(0, tc, body, 0, unroll=2)

    def wait_tile(tile, sl):
        def body(r, _):
            for k in range(TOP_K):
                row_copy(tile, r, k, sl).wait()
            return 0
        lax.fori_loop(0, tc, body, 0, unroll=2)

    @pl.when(t == 0)
    def _():
        start_tile(0, 0)

    @pl.when(t + 1 < pl.num_programs(0))
    def _():
        start_tile(t + 1, 1 - slot)

    wait_tile(t, slot)
    z = DEEPNORM_ALPHA * x_ref[...]
    for k in range(TOP_K):
        z = z + wt_ref[:, k:k + 1] * buf[slot, k]
    y = _layer_norm_rows(z, g_ref[...], b_ref[...])
    o_ref[...] = y
    obf_ref[...] = y.astype(obf_ref.dtype)


def combine_ln(ys, dest, wt, x, g, b):
    n, dm = x.shape
    tc = _pick(n, 128)
    row = lambda w: pl.BlockSpec((tc, w), lambda t, d: (t, 0))
    const = lambda w: pl.BlockSpec((1, w), lambda t, d: (0, 0))
    return pl.pallas_call(
        functools.partial(_combine_ln_kernel, tc=tc),
        grid_spec=pltpu.PrefetchScalarGridSpec(
            num_scalar_prefetch=1,
            grid=(n // tc,),
            in_specs=[pl.BlockSpec(memory_space=pl.ANY), row(TOP_K), row(dm), const(dm), const(dm)],
            out_specs=[row(dm), row(dm)],
            scratch_shapes=[pltpu.VMEM((2, TOP_K, tc, dm), jnp.float32),
                            pltpu.SemaphoreType.DMA((2,))],
        ),
        out_shape=[jax.ShapeDtypeStruct((n, dm), jnp.float32),
                   jax.ShapeDtypeStruct((n, dm), jnp.bfloat16)],
        compiler_params=_cparams("arbitrary"),
        name="combine_ln",
    )(dest.reshape(-1), ys, wt, x, g.reshape(1, dm), b.reshape(1, dm))


def _rope_tables(positions):
    inv = 1.0 / (ROPE_THETA ** (jnp.arange(0, ROT_DIM, 2, dtype=jnp.float32) / ROT_DIM))
    ang = positions.astype(jnp.float32).reshape(-1, 1) * inv
    cos, sin = jnp.cos(ang), jnp.sin(ang)
    n = ang.shape[0]
    rest = LANES - ROT_DIM
    c = jnp.concatenate([cos, cos, jnp.ones((n, rest), jnp.float32)], axis=1)
    s = jnp.concatenate([-sin, sin, jnp.zeros((n, rest), jnp.float32)], axis=1)
    return c, s


def moe_ln(z, ln_g, ln_b, router_w, router_b, wg, bg, wu, bu, wd, layer, bd, ln2_g, ln2_b,
           tm=256):
    n, dm = z.shape
    e = router_w.shape[1]
    x1, idx, wt, rank, cnt = ln_router(z, ln_g, ln_b, router_w, router_b)
    counts = cnt[0, :e]
    padded = (counts + tm - 1) // tm * tm
    pend = jnp.cumsum(padded)
    pstart = pend - padded
    dest = pstart[idx] + rank
    nblk = (n * TOP_K) // tm + e
    tok = jnp.broadcast_to(jnp.arange(n, dtype=jnp.int32)[:, None], (n, TOP_K))
    row_tok = jnp.zeros((nblk * tm,), jnp.int32).at[dest.reshape(-1)].set(tok.reshape(-1))
    starts = jnp.arange(nblk, dtype=jnp.int32) * tm
    block_exp = jnp.minimum(jnp.sum(pend[None, :] <= starts[:, None], axis=1),
                            e - 1).astype(jnp.int32)
    n_used = (pend[-1:] // tm).astype(jnp.int32)
    ys = expert_ffn(x1, row_tok, block_exp, n_used, wg, bg, wu, bu, wd, layer, bd, tm)
    return combine_ln(ys, dest.astype(jnp.int32), wt, x1, ln2_g, ln2_b)


def kernel(x, positions, diff_w_qkv, diff_lambda_q1, diff_lambda_k1, diff_lambda_q2, diff_lambda_k2, diff_subln_g, diff_w_o, moba_w_qkv, moba_w_o, ln_mix_g, ln_mix_b, router_w, router_b, exp_w_gate, exp_b_gate, exp_w_up, exp_b_up, exp_w_down, exp_b_down, ln_ffn_g, ln_ffn_b):
    b, s, dm = x.shape
    n = b * s
    bf = jnp.bfloat16
    rope_c, rope_s = _rope_tables(positions)
    xf = x.reshape(n, dm)
    x_bf = xf.astype(bf)
    n_mixers = 2
    for i in range(DEPTH):
        j = i // n_mixers
        if i % n_mixers == 0:
            heads = dm // DIFF_V_DIM
            lambda_init = 0.8 - 0.6 * math.exp(-0.3 * i)
            qkv = qkv_proj(x_bf, diff_w_qkv, j, rope_c, rope_s, heads * DIFF_V_DIM)
            att = diff_attention(qkv.reshape(b, s, -1), diff_lambda_q1[j], diff_lambda_k1[j],
                                 diff_lambda_q2[j], diff_lambda_k2[j], diff_subln_g[j],
                                 heads, lambda_init)
            w_o = diff_w_o
        else:
            heads = dm // HEAD_DIM
            qkv = qkv_proj(x_bf, moba_w_qkv, j, rope_c, rope_s, heads * HEAD_DIM)
            att = moba_attention(qkv.reshape(b, s, -1), heads)
            w_o = moba_w_o
        z = proj_resid(att.reshape(n, -1), w_o, j, xf)
        xf, x_bf = moe_ln(z, ln_mix_g[i], ln_mix_b[i], router_w[i], router_b[i],
                          exp_w_gate[i].astype(bf), exp_b_gate[i], exp_w_up[i].astype(bf),
                          exp_b_up[i], exp_w_down, i, exp_b_down[i],
                          ln_ffn_g[i], ln_ffn_b[i])
    return xf.reshape(b, s, dm)
```

```python
import functools
import math

import jax
import jax.numpy as jnp
from jax import lax
from jax.experimental import pallas as pl
from jax.experimental.pallas import tpu as pltpu

DEPTH = 2
ROPE_THETA = 500000.0
HEAD_DIM = 128
ROT_DIM = HEAD_DIM // 4
ROT_HALF = ROT_DIM // 2
DIFF_V_DIM = 2 * HEAD_DIM
MOBA_BLOCK = 256
MOBA_TOPK = 3
TOP_K = 4
SWIGLU_LIMIT = 7.0
SWIGLU_ALPHA = 1.702
LN_EPS = 1e-5
RMS_EPS = 1e-5
DEEPNORM_ALPHA = (2 * DEPTH) ** 0.25

LANES = 128
QK_FOLD = HEAD_DIM ** -0.5 * math.log2(math.e)
NEG_BIG = -1e30
NT_DIMS = (((1,), (1,)), ((), ()))
VMEM_LIMIT = 56 * 1024 * 1024


def _cparams(*sem):
    return pltpu.CompilerParams(dimension_semantics=sem, vmem_limit_bytes=VMEM_LIMIT)


def _pick(n, pref):
    t = min(pref, n)
    while n % t:
        t //= 2
    return t


def _cast_weight_tile(w_ref, wbf):
    @pl.when(pl.program_id(1) == 0)
    def _():
        wbf[...] = w_ref[0].astype(wbf.dtype)


def _qkv_kernel(x_ref, w_ref, c_ref, s_ref, o_ref, wbf, *, n_q_tiles, n_rope_tiles):
    j = pl.program_id(0)
    _cast_weight_tile(w_ref, wbf)
    acc = jnp.dot(x_ref[...], wbf[...], preferred_element_type=jnp.float32)
    tn = acc.shape[1]

    @pl.when(j < n_rope_tiles)
    def _():
        fold = jnp.where(j < n_q_tiles, QK_FOLD, 1.0)
        c = c_ref[...] * fold
        s = s_ref[...] * fold
        lane = lax.broadcasted_iota(jnp.int32, c.shape, 1)
        for h in range(tn // LANES):
            a = acc[:, h * LANES:(h + 1) * LANES]
            partner = jnp.where(lane < ROT_HALF,
                                pltpu.roll(a, LANES - ROT_HALF, 1),
                                pltpu.roll(a, ROT_HALF, 1))
            o_ref[:, h * LANES:(h + 1) * LANES] = (a * c + partner * s).astype(o_ref.dtype)

    @pl.when(j >= n_rope_tiles)
    def _():
        o_ref[...] = acc.astype(o_ref.dtype)


def qkv_proj(x_bf, w, layer, rope_c, rope_s, n_q_cols):
    m, k = x_bf.shape
    n = w.shape[2]
    tm = _pick(m, 1024)
    tn = _pick(math.gcd(n, n_q_cols), 512)
    return pl.pallas_call(
        functools.partial(_qkv_kernel, n_q_tiles=n_q_cols // tn, n_rope_tiles=2 * n_q_cols // tn),
        grid=(n // tn, m // tm),
        in_specs=[
            pl.BlockSpec((tm, k), lambda j, i: (i, 0)),
            pl.BlockSpec((1, k, tn), lambda j, i: (layer, 0, j)),
            pl.BlockSpec((tm, LANES), lambda j, i: (i, 0)),
            pl.BlockSpec((tm, LANES), lambda j, i: (i, 0)),
        ],
        out_specs=pl.BlockSpec((tm, tn), lambda j, i: (i, j)),
        out_shape=jax.ShapeDtypeStruct((m, n), jnp.bfloat16),
        scratch_shapes=[pltpu.VMEM((k, tn), jnp.bfloat16)],
        compiler_params=_cparams("parallel", "arbitrary"),
        name="qkv_proj",
    )(x_bf, w, rope_c, rope_s)


def _diff_attn_kernel(lq1_ref, lk1_ref, lq2_ref, lk2_ref, g_ref, q_ref, k_ref, v_ref,
                      o_ref, acc1, acc2, s_a, s_b, *, tq, lambda_init):
    i = pl.program_id(2)
    d = HEAD_DIM
    q = q_ref[0]
    qs = (q[:, :d], q[:, d:])
    accs = (acc1, acc2)

    def qk(j, sbuf):
        kj = k_ref[0, pl.ds(pl.multiple_of(j * tq, tq), tq), :]
        for c in range(2):
            sbuf[c] = lax.dot_general(qs[c], kj[:, c * d:(c + 1) * d], NT_DIMS,
                                      preferred_element_type=jnp.float32)

    def softmax_pv(j, sbuf, carry, masked):
        vj = v_ref[0, pl.ds(pl.multiple_of(j * tq, tq), tq), :]
        new = []
        for c in range(2):
            m_old, l_old = carry[2 * c], carry[2 * c + 1]
            s = sbuf[c]
            if masked:
                row = lax.broadcasted_iota(jnp.int32, s.shape, 0)
                col = lax.broadcasted_iota(jnp.int32, s.shape, 1)
                s = jnp.where(col <= row, s, NEG_BIG)
            m_new = jnp.maximum(m_old, jnp.max(s, axis=1, keepdims=True))
            alpha = jnp.exp2(m_old - m_new)
            p = jnp.exp2(s - m_new)
            l_new = alpha * l_old + jnp.sum(p, axis=1, keepdims=True)
            accs[c][...] = alpha * accs[c][...] + jnp.dot(
                p.astype(vj.dtype), vj, preferred_element_type=jnp.float32)
            new += [m_new, l_new]
        return tuple(new)

    acc1[...] = jnp.zeros_like(acc1)
    acc2[...] = jnp.zeros_like(acc2)
    m0 = jnp.full((tq, 1), NEG_BIG, jnp.float32)
    l0 = jnp.zeros((tq, 1), jnp.float32)

    qk(0, s_a)

    def pair(t, carry):
        j = 2 * t
        qk(j + 1, s_b)
        carry = softmax_pv(j, s_a, carry, False)
        qk(j + 2, s_a)
        return softmax_pv(j + 1, s_b, carry, False)

    carry = lax.fori_loop(0, i // 2, pair, (m0, l0, m0, l0))

    def last_even(carry):
        return softmax_pv(i, s_a, carry, True)

    def last_odd(carry):
        qk(i, s_b)
        carry = softmax_pv(i - 1, s_a, carry, False)
        return softmax_pv(i, s_b, carry, True)

    m1, l1, m2, l2 = lax.cond(i % 2 == 0, last_even, last_odd, carry)

    lam = (jnp.exp(jnp.sum(lq1_ref[...] * lk1_ref[...], axis=1, keepdims=True))
           - jnp.exp(jnp.sum(lq2_ref[...] * lk2_ref[...], axis=1, keepdims=True))
           + lambda_init)
    o = acc1[...] / l1 - lam * (acc2[...] / l2)
    o = o * lax.rsqrt(jnp.mean(o * o, axis=1, keepdims=True) + RMS_EPS)
    o = o * g_ref[...] * (1.0 - lambda_init)
    o_ref[0] = o.astype(o_ref.dtype)


def diff_attention(qkv, lq1, lk1, lq2, lk2, subln_g, n_heads, lambda_init):
    b, s, _ = qkv.shape
    dv = DIFF_V_DIM
    tq = _pick(s, 512)
    vec = lambda a: a.reshape(1, -1).astype(jnp.float32)
    small = lambda n: pl.BlockSpec((1, n), lambda bi, h, i: (0, 0))
    return pl.pallas_call(
        functools.partial(_diff_attn_kernel, tq=tq, lambda_init=lambda_init),
        grid=(b, n_heads, s // tq),
        in_specs=[
            small(HEAD_DIM), small(HEAD_DIM), small(HEAD_DIM), small(HEAD_DIM), small(dv),
            pl.BlockSpec((1, tq, dv), lambda bi, h, i: (bi, i, h)),
            pl.BlockSpec((1, s, dv), lambda bi, h, i: (bi, 0, n_heads + h)),
            pl.BlockSpec((1, s, dv), lambda bi, h, i: (bi, 0, 2 * n_heads + h)),
        ],
        out_specs=pl.BlockSpec((1, tq, dv), lambda bi, h, i: (bi, i, h)),
        out_shape=jax.ShapeDtypeStruct((b, s, n_heads * dv), jnp.bfloat16),
        scratch_shapes=[pltpu.VMEM((tq, dv), jnp.float32), pltpu.VMEM((tq, dv), jnp.float32),
                        pltpu.VMEM((2, tq, tq), jnp.float32), pltpu.VMEM((2, tq, tq), jnp.float32)],
        compiler_params=_cparams("parallel", "parallel", "arbitrary"),
        name="diff_attention",
    )(vec(lq1), vec(lk1), vec(lq2), vec(lk2), vec(subln_g), qkv, qkv, qkv)


def _moba_kernel(q_ref, k_ref, v_ref, o_ref, kaug, vaug, kmean, cmask, acc, s_a, s_b,
                 *, nb, ck, qb):
    t = pl.program_id(2)
    blk = MOBA_BLOCK
    d = HEAD_DIM
    tq = qb * blk
    s_len = nb * blk
    n_chunks = s_len // ck
    shift = blk.bit_length() - 1

    @pl.when(t == 0)
    def _():
        vaug[:, 0:d] = v_ref[0]
        vaug[:, d:2 * d] = jnp.ones((s_len, d), vaug.dtype)
        row = lax.broadcasted_iota(jnp.int32, (tq, ck), 0)
        col = lax.broadcasted_iota(jnp.int32, (tq, ck), 1)
        ahead = (col & (blk - 1)) > (row & (blk - 1))
        for pos in range(ck // tq):
            same_blk = (col >> shift) == (pos * qb + (row >> shift))
            cmask[pos] = jnp.where(same_blk & ahead, NEG_BIG, 0.0)
        r = lax.broadcasted_iota(jnp.int32, (LANES, s_len), 0)
        c = lax.broadcasted_iota(jnp.int32, (LANES, s_len), 1)
        ind = jnp.where((c >> shift) == r, 1.0, 0.0).astype(jnp.bfloat16)
        km = jnp.dot(ind, k_ref[0], preferred_element_type=jnp.float32) * (1.0 / blk)
        hi = km.astype(jnp.bfloat16)
        kmean[0:LANES, :] = hi
        kmean[LANES:2 * LANES, :] = (km - hi.astype(jnp.float32)).astype(jnp.bfloat16)
        rr = lax.broadcasted_iota(jnp.int32, (s_len, LANES), 0)
        cc = lax.broadcasted_iota(jnp.int32, (s_len, LANES), 1)
        kaug[:, 0:d] = k_ref[0]
        kaug[:, d:d + LANES] = jnp.where((rr >> shift) == cc, 1.0, 0.0).astype(jnp.bfloat16)

    q = q_ref[0]
    nbp = -(-nb // 8) * 8
    g2 = lax.dot_general(kmean[...], q, NT_DIMS, preferred_element_type=jnp.float32)
    gate = g2[0:nbp] + g2[LANES:LANES + nbp]
    rowf = lax.broadcasted_iota(jnp.int32, gate.shape, 0).astype(jnp.float32)
    own_blk = (t * qb + (lax.broadcasted_iota(jnp.int32, (1, tq), 1) >> shift)).astype(jnp.float32)
    gate = jnp.where(rowf < own_blk, gate, -jnp.inf)
    sel = jnp.zeros(gate.shape, jnp.float32)
    for _ in range(MOBA_TOPK):
        mx = jnp.max(gate, axis=0, keepdims=True)
        first = jnp.min(jnp.where(gate == mx, rowf, float(LANES)), axis=0, keepdims=True)
        hit = (rowf == first) & (mx > -jnp.inf)
        sel = jnp.where(hit, 1.0, sel)
        gate = jnp.where(rowf == first, -jnp.inf, gate)
    bias_t = jnp.where((sel > 0.0) | (rowf == own_blk) | (rowf >= float(nb)), 0.0, NEG_BIG)
    if nbp < LANES:
        bias_t = jnp.concatenate([bias_t, jnp.zeros((LANES - nbp, tq), jnp.float32)], axis=0)
    q_aug = jnp.concatenate([q, bias_t.T.astype(jnp.bfloat16)], axis=1)

    def qk(c, sbuf):
        kc = kaug[pl.ds(pl.multiple_of(c * ck, ck), ck), :]
        sbuf[...] = lax.dot_general(q_aug, kc, NT_DIMS, preferred_element_type=jnp.float32)

    def softmax_pv(c, sbuf, m_old, causal):
        s = sbuf[...]
        if causal:
            s = s + cmask[t % (ck // tq)]
        m_new = jnp.maximum(m_old, jnp.max(s, axis=1, keepdims=True))
        p = jnp.exp2(s - m_new).astype(vaug.dtype)
        vc = vaug[pl.ds(pl.multiple_of(c * ck, ck), ck), :]
        acc[...] = jnp.exp2(m_old - m_new) * acc[...] + jnp.dot(
            p, vc, preferred_element_type=jnp.float32)
        return m_new

    nc = (t * tq + tq + ck - 1) // ck
    acc[...] = jnp.zeros_like(acc)
    qk(nc - 1, s_a)
    qk(0, s_b)
    m_run = softmax_pv(nc - 1, s_a, jnp.full((tq, 1), NEG_BIG, jnp.float32), True)

    def pair(u, m_run):
        c = 2 * u
        qk(c + 1, s_a)
        m_run = softmax_pv(c, s_b, m_run, False)
        qk(jnp.minimum(c + 2, n_chunks - 1), s_b)
        return softmax_pv(c + 1, s_a, m_run, False)

    m_run = lax.fori_loop(0, (nc - 1) // 2, pair, m_run)

    @pl.when((nc - 1) % 2 == 1)
    def _():
        softmax_pv(nc - 2, s_b, m_run, False)

    o_ref[0] = (acc[:, 0:d] / acc[:, d:2 * d]).astype(o_ref.dtype)


def moba_attention(qkv, n_heads):
    b, s, _ = qkv.shape
    d = HEAD_DIM
    blk = MOBA_BLOCK
    assert s % blk == 0 and blk & (blk - 1) == 0
    nb = s // blk
    assert nb <= LANES
    ck = _pick(s, 4 * blk)
    qb = math.gcd(nb, ck // blk)
    tq = qb * blk
    return pl.pallas_call(
        functools.partial(_moba_kernel, nb=nb, ck=ck, qb=qb),
        grid=(b, n_heads, nb // qb),
        in_specs=[
            pl.BlockSpec((1, tq, d), lambda bi, h, i: (bi, i, h)),
            pl.BlockSpec((1, s, d), lambda bi, h, i: (bi, 0, n_heads + h)),
            pl.BlockSpec((1, s, d), lambda bi, h, i: (bi, 0, 2 * n_heads + h)),
        ],
        out_specs=pl.BlockSpec((1, tq, d), lambda bi, h, i: (bi, i, h)),
        out_shape=jax.ShapeDtypeStruct((b, s, n_heads * d), jnp.bfloat16),
        scratch_shapes=[pltpu.VMEM((s, d + LANES), jnp.bfloat16),
                        pltpu.VMEM((s, 2 * d), jnp.bfloat16),
                        pltpu.VMEM((2 * LANES, d), jnp.bfloat16),
                        pltpu.VMEM((ck // tq, tq, ck), jnp.float32),
                        pltpu.VMEM((tq, 2 * d), jnp.float32),
                        pltpu.VMEM((tq, ck), jnp.float32),
                        pltpu.VMEM((tq, ck), jnp.float32)],
        compiler_params=_cparams("parallel", "parallel", "arbitrary"),
        name="moba_attention",
    )(qkv, qkv, qkv)


def _proj_resid_kernel(a_ref, w_ref, r_ref, o_ref, wbf):
    _cast_weight_tile(w_ref, wbf)
    o_ref[...] = DEEPNORM_ALPHA * r_ref[...] + jnp.dot(
        a_ref[...], wbf[...], preferred_element_type=jnp.float32)


def proj_resid(a_bf, w, layer, resid):
    m, k = a_bf.shape
    n = w.shape[2]
    tm = _pick(m, 1024)
    tn = _pick(n, 512)
    return pl.pallas_call(
        _proj_resid_kernel,
        grid=(n // tn, m // tm),
        in_specs=[
            pl.BlockSpec((tm, k), lambda j, i: (i, 0)),
            pl.BlockSpec((1, k, tn), lambda j, i: (layer, 0, j)),
            pl.BlockSpec((tm, tn), lambda j, i: (i, j)),
        ],
        out_specs=pl.BlockSpec((tm, tn), lambda j, i: (i, j)),
        out_shape=jax.ShapeDtypeStruct((m, n), jnp.float32),
        scratch_shapes=[pltpu.VMEM((k, tn), jnp.bfloat16)],
        compiler_params=_cparams("parallel", "arbitrary"),
        name="proj_resid",
    )(a_bf, w, resid)


def _layer_norm_rows(z, g, b):
    mu = jnp.mean(z, axis=1, keepdims=True)
    zc = z - mu
    var = jnp.mean(zc * zc, axis=1, keepdims=True)
    return zc * lax.rsqrt(var + LN_EPS) * g + b


def _ln_router_kernel(z_ref, g_ref, b_ref, rw_ref, rb_ref,
                      y_ref, idx_ref, wt_ref, rank_ref, cnt_ref, carry):
    t = pl.program_id(0)

    @pl.when(t == 0)
    def _():
        carry[...] = jnp.zeros_like(carry)

    y = _layer_norm_rows(z_ref[...], g_ref[...], b_ref[...])
    y_ref[...] = y
    logits = jnp.dot(y, rw_ref[...], preferred_element_type=jnp.float32,
                     precision=lax.Precision.HIGHEST) + rb_ref[...]
    tm = logits.shape[0]
    lanef = lax.broadcasted_iota(jnp.int32, logits.shape, 1).astype(jnp.float32)
    vals, idxs = [], []
    onehot = jnp.zeros(logits.shape, jnp.float32)
    for _ in range(TOP_K):
        mx = jnp.max(logits, axis=1, keepdims=True)
        first = jnp.min(jnp.where(logits == mx, lanef, float(LANES)), axis=1, keepdims=True)
        hit = lanef == first
        onehot = jnp.where(hit, 1.0, onehot)
        logits = jnp.where(hit, -jnp.inf, logits)
        vals.append(mx)
        idxs.append(first)
    es = [jnp.exp(v - vals[0]) for v in vals]
    den = es[0] + es[1] + es[2] + es[3]

    r = lax.broadcasted_iota(jnp.int32, (tm, tm), 0)
    c = lax.broadcasted_iota(jnp.int32, (tm, tm), 1)
    tri = jnp.where(c < r, 1.0, 0.0).astype(jnp.bfloat16)
    before = jnp.dot(tri, onehot.astype(jnp.bfloat16),
                     preferred_element_type=jnp.float32) + carry[...]
    for k in range(TOP_K):
        rk = jnp.sum(jnp.where(lanef == idxs[k], before, 0.0), axis=1, keepdims=True)
        rank_ref[:, k:k + 1] = rk.astype(jnp.int32)
        idx_ref[:, k:k + 1] = idxs[k].astype(jnp.int32)
        wt_ref[:, k:k + 1] = es[k] / den
    carry[...] = carry[...] + jnp.sum(onehot, axis=0, keepdims=True)
    cnt_ref[...] = carry[...].astype(jnp.int32)


def ln_router(z, g, b, router_w, router_b):
    n, dm = z.shape
    e = router_w.shape[1]
    assert TOP_K <= e <= LANES
    rw = jnp.zeros((dm, LANES), jnp.float32).at[:, :e].set(router_w.astype(jnp.float32))
    rb = jnp.full((1, LANES), NEG_BIG, jnp.float32).at[0, :e].set(router_b.astype(jnp.float32))
    tm = _pick(n, 256)
    row = lambda w: pl.BlockSpec((tm, w), lambda t: (t, 0))
    const = lambda r, w: pl.BlockSpec((r, w), lambda t: (0, 0))
    return pl.pallas_call(
        _ln_router_kernel,
        grid=(n // tm,),
        in_specs=[row(dm), const(1, dm), const(1, dm), const(dm, LANES), const(1, LANES)],
        out_specs=[row(dm), row(TOP_K), row(TOP_K), row(TOP_K), const(1, LANES)],
        out_shape=[
            jax.ShapeDtypeStruct((n, dm), jnp.float32),
            jax.ShapeDtypeStruct((n, TOP_K), jnp.int32),
            jax.ShapeDtypeStruct((n, TOP_K), jnp.float32),
            jax.ShapeDtypeStruct((n, TOP_K), jnp.int32),
            jax.ShapeDtypeStruct((1, LANES), jnp.int32),
        ],
        scratch_shapes=[pltpu.VMEM((1, LANES), jnp.float32)],
        compiler_params=_cparams("arbitrary"),
        name="ln_router",
    )(z, g.reshape(1, dm), b.reshape(1, dm), rw, rb)


def _expert_up_kernel(be_ref, nu_ref, tok_ref, x_hbm, wg_ref, bg_ref, wu_ref, bu_ref, h_ref,
                      xbuf, wg_bf, wu_bf, sem, *, tm):
    t = pl.program_id(0)
    n_used = nu_ref[0]
    slot = t % 2

    @pl.when((t == 0) | (be_ref[t] != be_ref[jnp.maximum(t - 1, 0)]))
    def _():
        wg_bf[...] = wg_ref[0, 0].astype(wg_bf.dtype)
        wu_bf[...] = wu_ref[0, 0].astype(wu_bf.dtype)

    def row_copy(blk, r, sl):
        return pltpu.make_async_copy(x_hbm.at[pl.ds(tok_ref[blk * tm + r], 1)],
                                     xbuf.at[sl, pl.ds(r, 1)], sem.at[sl])

    def start_block(blk, sl):
        def body(r, _):
            row_copy(blk, r, sl).start()
            return 0
        lax.fori_loop(0, tm, body, 0, unroll=8)

    def wait_block(blk, sl):
        def body(r, _):
            row_copy(blk, r, sl).wait()
            return 0
        lax.fori_loop(0, tm, body, 0, unroll=8)

    @pl.when((t == 0) & (n_used > 0))
    def _():
        start_block(0, 0)

    @pl.when(t < n_used)
    def _():
        wait_block(t, slot)
        x = xbuf[slot].astype(jnp.bfloat16)
        nxt = jnp.minimum(t + 1, n_used - 1)
        for r in range(tm):
            row_copy(nxt, r, 1 - slot).start()
        g = jnp.dot(x, wg_bf[...], preferred_element_type=jnp.float32) + bg_ref[0]
        u = jnp.dot(x, wu_bf[...], preferred_element_type=jnp.float32) + bu_ref[0]
        g = jnp.minimum(g, SWIGLU_LIMIT)
        u = jnp.clip(u, -SWIGLU_LIMIT, SWIGLU_LIMIT)
        h = g * jax.nn.sigmoid(SWIGLU_ALPHA * g) * (u + 1.0)
        h_ref[...] = h.astype(h_ref.dtype)

    @pl.when(t + 1 == n_used)
    def _():
        wait_block(t, 1 - slot)

    @pl.when(t >= n_used)
    def _():
        h_ref[...] = jnp.zeros_like(h_ref)


def _expert_down_kernel(be_ref, nu_ref, h_ref, wd_ref, bd_ref, y_ref, wbf):
    t = pl.program_id(0)

    @pl.when((t == 0) | (be_ref[t] != be_ref[jnp.maximum(t - 1, 0)]))
    def _():
        wbf[...] = wd_ref[0, 0].astype(wbf.dtype)

    @pl.when(t < nu_ref[0])
    def _():
        y_ref[...] = jnp.dot(h_ref[...], wbf[...],
                             preferred_element_type=jnp.float32) + bd_ref[0]

    @pl.when(t >= nu_ref[0])
    def _():
        y_ref[...] = jnp.zeros_like(y_ref)


def expert_ffn(x, row_tok, block_exp, n_used, wg, bg, wu, bu, wd, layer, bd, tm):
    p = row_tok.shape[0]
    dm = x.shape[1]
    _, e, _, f = wg.shape
    nblk = p // tm
    w_spec = pl.BlockSpec((1, 1, dm, f), lambda t, be, nu, tok: (layer, be[t], 0, 0),
                          pipeline_mode=pl.Buffered(1))
    h = pl.pallas_call(
        functools.partial(_expert_up_kernel, tm=tm),
        grid_spec=pltpu.PrefetchScalarGridSpec(
            num_scalar_prefetch=3,
            grid=(nblk,),
            in_specs=[
                pl.BlockSpec(memory_space=pl.ANY),
                w_spec,
                pl.BlockSpec((1, 1, f), lambda t, be, nu, tok: (be[t], 0, 0)),
                w_spec,
                pl.BlockSpec((1, 1, f), lambda t, be, nu, tok: (be[t], 0, 0)),
            ],
            out_specs=pl.BlockSpec((tm, f), lambda t, be, nu, tok: (t, 0)),
            scratch_shapes=[pltpu.VMEM((2, tm, dm), x.dtype),
                            pltpu.VMEM((dm, f), jnp.bfloat16), pltpu.VMEM((dm, f), jnp.bfloat16),
                            pltpu.SemaphoreType.DMA((2,))],
        ),
        out_shape=jax.ShapeDtypeStruct((p, f), jnp.bfloat16),
        compiler_params=_cparams("arbitrary"),
        name="expert_up",
    )(block_exp, n_used, row_tok, x, wg, bg.reshape(e, 1, f), wu, bu.reshape(e, 1, f))
    return pl.pallas_call(
        _expert_down_kernel,
        grid_spec=pltpu.PrefetchScalarGridSpec(
            num_scalar_prefetch=2,
            grid=(nblk,),
            in_specs=[
                pl.BlockSpec((tm, f), lambda t, be, nu: (t, 0)),
                pl.BlockSpec((1, 1, f, dm), lambda t, be, nu: (layer, be[t], 0, 0)),
                pl.BlockSpec((1, 1, dm), lambda t, be, nu: (be[t], 0, 0)),
            ],
            out_specs=pl.BlockSpec((tm, dm), lambda t, be, nu: (t, 0)),
            scratch_shapes=[pltpu.VMEM((f, dm), jnp.bfloat16)],
        ),
        out_shape=jax.ShapeDtypeStruct((p, dm), jnp.float32),
        compiler_params=_cparams("arbitrary"),
        name="expert_down",
    )(block_exp, n_used, h, wd, bd.reshape(e, 1, dm))


def _combine_ln_kernel(dest_ref, ys_hbm, wt_ref, x_ref, g_ref, b_ref, o_ref, obf_ref,
                       buf, sem, *, tc):
    t = pl.program_id(0)
    slot = t % 2

    def row_copy(tile, r, k, sl):
        return pltpu.make_async_copy(
            ys_hbm.at[pl.ds(dest_ref[(tile * tc + r) * TOP_K + k], 1)],
            buf.at[sl, k, pl.ds(r, 1)], sem.at[sl])

    def start_tile(tile, sl):
        def body(r, _):
            for k in range(TOP_K):
                row_copy(tile, r, k, sl).start()
            return 0
        lax.fori_loop(0, tc, body, 0, unroll=2)

    def wait_tile(tile, sl):
        def body(r, _):
            for k in range(TOP_K):
                row_copy(tile, r, k, sl).wait()
            return 0
        lax.fori_loop(0, tc, body, 0, unroll=2)

    @pl.when(t == 0)
    def _():
        start_tile(0, 0)

    @pl.when(t + 1 < pl.num_programs(0))
    def _():
        start_tile(t + 1, 1 - slot)

    wait_tile(t, slot)
    z = DEEPNORM_ALPHA * x_ref[...]
    for k in range(TOP_K):
        z = z + wt_ref[:, k:k + 1] * buf[slot, k]
    y = _layer_norm_rows(z, g_ref[...], b_ref[...])
    o_ref[...] = y
    obf_ref[...] = y.astype(obf_ref.dtype)


def combine_ln(ys, dest, wt, x, g, b):
    n, dm = x.shape
    tc = _pick(n, 128)
    row = lambda w: pl.BlockSpec((tc, w), lambda t, d: (t, 0))
    const = lambda w: pl.BlockSpec((1, w), lambda t, d: (0, 0))
    return pl.pallas_call(
        functools.partial(_combine_ln_kernel, tc=tc),
        grid_spec=pltpu.PrefetchScalarGridSpec(
            num_scalar_prefetch=1,
            grid=(n // tc,),
            in_specs=[pl.BlockSpec(memory_space=pl.ANY), row(TOP_K), row(dm), const(dm), const(dm)],
            out_specs=[row(dm), row(dm)],
            scratch_shapes=[pltpu.VMEM((2, TOP_K, tc, dm), jnp.float32),
                            pltpu.SemaphoreType.DMA((2,))],
        ),
        out_shape=[jax.ShapeDtypeStruct((n, dm), jnp.float32),
                   jax.ShapeDtypeStruct((n, dm), jnp.bfloat16)],
        compiler_params=_cparams("arbitrary"),
        name="combine_ln",
    )(dest.reshape(-1), ys, wt, x, g.reshape(1, dm), b.reshape(1, dm))


def _rope_tables(positions):
    inv = 1.0 / (ROPE_THETA ** (jnp.arange(0, ROT_DIM, 2, dtype=jnp.float32) / ROT_DIM))
    ang = positions.astype(jnp.float32).reshape(-1, 1) * inv
    cos, sin = jnp.cos(ang), jnp.sin(ang)
    n = ang.shape[0]
    rest = LANES - ROT_DIM
    c = jnp.concatenate([cos, cos, jnp.ones((n, rest), jnp.float32)], axis=1)
    s = jnp.concatenate([-sin, sin, jnp.zeros((n, rest), jnp.float32)], axis=1)
    return c, s


def moe_ln(z, ln_g, ln_b, router_w, router_b, wg, bg, wu, bu, wd, layer, bd, ln2_g, ln2_b,
           tm=256):
    n, dm = z.shape
    e = router_w.shape[1]
    x1, idx, wt, rank, cnt = ln_router(z, ln_g, ln_b, router_w, router_b)
    counts = cnt[0, :e]
    padded = (counts + tm - 1) // tm * tm
    pend = jnp.cumsum(padded)
    pstart = pend - padded
    dest = pstart[idx] + rank
    nblk = (n * TOP_K) // tm + e
    tok = jnp.broadcast_to(jnp.arange(n, dtype=jnp.int32)[:, None], (n, TOP_K))
    row_tok = jnp.zeros((nblk * tm,), jnp.int32).at[dest.reshape(-1)].set(tok.reshape(-1))
    starts = jnp.arange(nblk, dtype=jnp.int32) * tm
    block_exp = jnp.minimum(jnp.sum(pend[None, :] <= starts[:, None], axis=1),
                            e - 1).astype(jnp.int32)
    n_used = (pend[-1:] // tm).astype(jnp.int32)
    ys = expert_ffn(x1, row_tok, block_exp, n_used, wg, bg, wu, bu, wd, layer, bd, tm)
    return combine_ln(ys, dest.astype(jnp.int32), wt, x1, ln2_g, ln2_b)


def kernel(x, positions, diff_w_qkv, diff_lambda_q1, diff_lambda_k1, diff_lambda_q2, diff_lambda_k2, diff_subln_g, diff_w_o, moba_w_qkv, moba_w_o, ln_mix_g, ln_mix_b, router_w, router_b, exp_w_gate, exp_b_gate, exp_w_up, exp_b_up, exp_w_down, exp_b_down, ln_ffn_g, ln_ffn_b):
    b, s, dm = x.shape
    n = b * s
    bf = jnp.bfloat16
    rope_c, rope_s = _rope_tables(positions)
    xf = x.reshape(n, dm)
    x_bf = xf.astype(bf)
    n_mixers = 2
    for i in range(DEPTH):
        j = i // n_mixers
        if i % n_mixers == 0:
            heads = dm // DIFF_V_DIM
            lambda_init = 0.8 - 0.6 * math.exp(-0.3 * i)
            qkv = qkv_proj(x_bf, diff_w_qkv, j, rope_c, rope_s, heads * DIFF_V_DIM)
            att = diff_attention(qkv.reshape(b, s, -1), diff_lambda_q1[j], diff_lambda_k1[j],
                                 diff_lambda_q2[j], diff_lambda_k2[j], diff_subln_g[j],
                                 heads, lambda_init)
            w_o = diff_w_o
        else:
            heads = dm // HEAD_DIM
            qkv = qkv_proj(x_bf, moba_w_qkv, j, rope_c, rope_s, heads * HEAD_DIM)
            att = moba_attention(qkv.reshape(b, s, -1), heads)
            w_o = moba_w_o
        z = proj_resid(att.reshape(n, -1), w_o, j, xf)
        xf, x_bf = moe_ln(z, ln_mix_g[i], ln_mix_b[i], router_w[i], router_b[i],
                          exp_w_gate, exp_b_gate[i], exp_w_up, exp_b_up[i], exp_w_down, i,
                          exp_b_down[i],
                          ln_ffn_g[i], ln_ffn_b[i])
    return xf.reshape(b, s, dm)
```

```python
import functools
import math

import jax
import jax.numpy as jnp
from jax import lax
from jax.experimental import pallas as pl
from jax.experimental.pallas import tpu as pltpu

DEPTH = 2
ROPE_THETA = 500000.0
HEAD_DIM = 128
ROT_DIM = HEAD_DIM // 4
ROT_HALF = ROT_DIM // 2
DIFF_V_DIM = 2 * HEAD_DIM
MOBA_BLOCK = 256
MOBA_TOPK = 3
TOP_K = 4
SWIGLU_LIMIT = 7.0
SWIGLU_ALPHA = 1.702
LN_EPS = 1e-5
RMS_EPS = 1e-5
DEEPNORM_ALPHA = (2 * DEPTH) ** 0.25

LANES = 128
QK_FOLD = HEAD_DIM ** -0.5 * math.log2(math.e)
NEG_BIG = -1e30
NT_DIMS = (((1,), (1,)), ((), ()))
EPILOGUE_ROWS = 256
VMEM_LIMIT = 56 * 1024 * 1024


def _cparams(*sem):
    return pltpu.CompilerParams(dimension_semantics=sem, vmem_limit_bytes=VMEM_LIMIT)


def _pick(n, pref):
    t = min(pref, n)
    while n % t:
        t //= 2
    return t


def _cast_weight_tile(w_ref, wbf):
    @pl.when(pl.program_id(1) == 0)
    def _():
        wbf[...] = w_ref[0].astype(wbf.dtype)


def _qkv_kernel(x_ref, w_ref, c_ref, s_ref, o_ref, wbf, *, n_q_tiles, n_rope_tiles):
    j = pl.program_id(0)
    _cast_weight_tile(w_ref, wbf)
    tm, tn = o_ref.shape
    rc = min(tm, EPILOGUE_ROWS)

    def row_chunks():
        for r in range(tm // rc):
            rows = pl.ds(r * rc, rc)
            yield rows, jnp.dot(x_ref[rows, :], wbf[...], preferred_element_type=jnp.float32)

    @pl.when(j < n_rope_tiles)
    def _():
        fold = jnp.where(j < n_q_tiles, QK_FOLD, 1.0)
        lane = lax.broadcasted_iota(jnp.int32, (rc, LANES), 1)
        for rows, acc in row_chunks():
            c = c_ref[rows, :] * fold
            s = s_ref[rows, :] * fold
            for h in range(tn // LANES):
                a = acc[:, h * LANES:(h + 1) * LANES]
                partner = jnp.where(lane < ROT_HALF,
                                    pltpu.roll(a, LANES - ROT_HALF, 1),
                                    pltpu.roll(a, ROT_HALF, 1))
                o_ref[rows, h * LANES:(h + 1) * LANES] = (a * c + partner * s).astype(o_ref.dtype)

    @pl.when(j >= n_rope_tiles)
    def _():
        for rows, acc in row_chunks():
            o_ref[rows, :] = acc.astype(o_ref.dtype)


def qkv_proj(x_bf, w, layer, rope_c, rope_s, n_q_cols):
    m, k = x_bf.shape
    n = w.shape[2]
    tm = _pick(m, 1024)
    tn = _pick(math.gcd(n, n_q_cols), 512)
    return pl.pallas_call(
        functools.partial(_qkv_kernel, n_q_tiles=n_q_cols // tn, n_rope_tiles=2 * n_q_cols // tn),
        grid=(n // tn, m // tm),
        in_specs=[
            pl.BlockSpec((tm, k), lambda j, i: (i, 0)),
            pl.BlockSpec((1, k, tn), lambda j, i: (layer, 0, j)),
            pl.BlockSpec((tm, LANES), lambda j, i: (i, 0)),
            pl.BlockSpec((tm, LANES), lambda j, i: (i, 0)),
        ],
        out_specs=pl.BlockSpec((tm, tn), lambda j, i: (i, j)),
        out_shape=jax.ShapeDtypeStruct((m, n), jnp.bfloat16),
        scratch_shapes=[pltpu.VMEM((k, tn), jnp.bfloat16)],
        compiler_params=_cparams("parallel", "arbitrary"),
        name="qkv_proj",
    )(x_bf, w, rope_c, rope_s)


def _diff_attn_kernel(lq1_ref, lk1_ref, lq2_ref, lk2_ref, g_ref, q_ref, k_ref, v_ref,
                      o_ref, acc1, acc2, s_a, s_b, *, tq, lambda_init):
    i = pl.program_id(2)
    d = HEAD_DIM
    q = q_ref[0]
    qs = (q[:, :d], q[:, d:])
    accs = (acc1, acc2)

    def qk(j, sbuf):
        kj = k_ref[0, pl.ds(pl.multiple_of(j * tq, tq), tq), :]
        for c in range(2):
            sbuf[c] = lax.dot_general(qs[c], kj[:, c * d:(c + 1) * d], NT_DIMS,
                                      preferred_element_type=jnp.float32)

    def softmax_pv(j, sbuf, carry, masked):
        vj = v_ref[0, pl.ds(pl.multiple_of(j * tq, tq), tq), :]
        new = []
        for c in range(2):
            m_old, l_old = carry[2 * c], carry[2 * c + 1]
            s = sbuf[c]
            if masked:
                row = lax.broadcasted_iota(jnp.int32, s.shape, 0)
                col = lax.broadcasted_iota(jnp.int32, s.shape, 1)
                s = jnp.where(col <= row, s, NEG_BIG)
            m_new = jnp.maximum(m_old, jnp.max(s, axis=1, keepdims=True))
            alpha = jnp.exp2(m_old - m_new)
            p = jnp.exp2(s - m_new)
            l_new = alpha * l_old + jnp.sum(p, axis=1, keepdims=True)
            accs[c][...] = alpha * accs[c][...] + jnp.dot(
                p.astype(vj.dtype), vj, preferred_element_type=jnp.float32)
            new += [m_new, l_new]
        return tuple(new)

    acc1[...] = jnp.zeros_like(acc1)
    acc2[...] = jnp.zeros_like(acc2)
    m0 = jnp.full((tq, 1), NEG_BIG, jnp.float32)
    l0 = jnp.zeros((tq, 1), jnp.float32)

    qk(0, s_a)

    def pair(t, carry):
        j = 2 * t
        qk(j + 1, s_b)
        carry = softmax_pv(j, s_a, carry, False)
        qk(j + 2, s_a)
        return softmax_pv(j + 1, s_b, carry, False)

    carry = lax.fori_loop(0, i // 2, pair, (m0, l0, m0, l0))

    def last_even(carry):
        return softmax_pv(i, s_a, carry, True)

    def last_odd(carry):
        qk(i, s_b)
        carry = softmax_pv(i - 1, s_a, carry, False)
        return softmax_pv(i, s_b, carry, True)

    m1, l1, m2, l2 = lax.cond(i % 2 == 0, last_even, last_odd, carry)

    lam = (jnp.exp(jnp.sum(lq1_ref[...] * lk1_ref[...], axis=1, keepdims=True))
           - jnp.exp(jnp.sum(lq2_ref[...] * lk2_ref[...], axis=1, keepdims=True))
           + lambda_init)
    o = acc1[...] / l1 - lam * (acc2[...] / l2)
    o = o * lax.rsqrt(jnp.mean(o * o, axis=1, keepdims=True) + RMS_EPS)
    o = o * g_ref[...] * (1.0 - lambda_init)
    o_ref[0] = o.astype(o_ref.dtype)


def diff_attention(qkv, lq1, lk1, lq2, lk2, subln_g, n_heads, lambda_init):
    b, s, _ = qkv.shape
    dv = DIFF_V_DIM
    tq = _pick(s, 512)
    vec = lambda a: a.reshape(1, -1).astype(jnp.float32)
    small = lambda n: pl.BlockSpec((1, n), lambda bi, h, i: (0, 0))
    return pl.pallas_call(
        functools.partial(_diff_attn_kernel, tq=tq, lambda_init=lambda_init),
        grid=(b, n_heads, s // tq),
        in_specs=[
            small(HEAD_DIM), small(HEAD_DIM), small(HEAD_DIM), small(HEAD_DIM), small(dv),
            pl.BlockSpec((1, tq, dv), lambda bi, h, i: (bi, i, h)),
            pl.BlockSpec((1, s, dv), lambda bi, h, i: (bi, 0, n_heads + h)),
            pl.BlockSpec((1, s, dv), lambda bi, h, i: (bi, 0, 2 * n_heads + h)),
        ],
        out_specs=pl.BlockSpec((1, tq, dv), lambda bi, h, i: (bi, i, h)),
        out_shape=jax.ShapeDtypeStruct((b, s, n_heads * dv), jnp.bfloat16),
        scratch_shapes=[pltpu.VMEM((tq, dv), jnp.float32), pltpu.VMEM((tq, dv), jnp.float32),
                        pltpu.VMEM((2, tq, tq), jnp.float32), pltpu.VMEM((2, tq, tq), jnp.float32)],
        compiler_params=_cparams("parallel", "parallel", "arbitrary"),
        name="diff_attention",
    )(vec(lq1), vec(lk1), vec(lq2), vec(lk2), vec(subln_g), qkv, qkv, qkv)


def _moba_kernel(q_ref, k_ref, v_ref, o_ref, kaug, vaug, kmean, cmask, acc, s_a, s_b,
                 *, nb, ck, qb):
    t = pl.program_id(2)
    blk = MOBA_BLOCK
    d = HEAD_DIM
    tq = qb * blk
    s_len = nb * blk
    n_chunks = s_len // ck
    shift = blk.bit_length() - 1

    @pl.when(t == 0)
    def _():
        vaug[:, 0:d] = v_ref[0]
        vaug[:, d:2 * d] = jnp.ones((s_len, d), vaug.dtype)
        row = lax.broadcasted_iota(jnp.int32, (tq, ck), 0)
        col = lax.broadcasted_iota(jnp.int32, (tq, ck), 1)
        ahead = (col & (blk - 1)) > (row & (blk - 1))
        for pos in range(ck // tq):
            same_blk = (col >> shift) == (pos * qb + (row >> shift))
            cmask[pos] = jnp.where(same_blk & ahead, NEG_BIG, 0.0)
        r = lax.broadcasted_iota(jnp.int32, (LANES, s_len), 0)
        c = lax.broadcasted_iota(jnp.int32, (LANES, s_len), 1)
        ind = jnp.where((c >> shift) == r, 1.0, 0.0).astype(jnp.bfloat16)
        km = jnp.dot(ind, k_ref[0], preferred_element_type=jnp.float32) * (1.0 / blk)
        hi = km.astype(jnp.bfloat16)
        kmean[0:LANES, :] = hi
        kmean[LANES:2 * LANES, :] = (km - hi.astype(jnp.float32)).astype(jnp.bfloat16)
        rr = lax.broadcasted_iota(jnp.int32, (s_len, LANES), 0)
        cc = lax.broadcasted_iota(jnp.int32, (s_len, LANES), 1)
        kaug[:, 0:d] = k_ref[0]
        kaug[:, d:d + LANES] = jnp.where((rr >> shift) == cc, 1.0, 0.0).astype(jnp.bfloat16)

    q = q_ref[0]
    nbp = -(-nb // 8) * 8
    g2 = lax.dot_general(kmean[...], q, NT_DIMS, preferred_element_type=jnp.float32)
    gate = g2[0:nbp] + g2[LANES:LANES + nbp]
    rowf = lax.broadcasted_iota(jnp.int32, gate.shape, 0).astype(jnp.float32)
    own_blk = (t * qb + (lax.broadcasted_iota(jnp.int32, (1, tq), 1) >> shift)).astype(jnp.float32)
    gate = jnp.where(rowf < own_blk, gate, -jnp.inf)
    sel = jnp.zeros(gate.shape, jnp.float32)
    for _ in range(MOBA_TOPK):
        mx = jnp.max(gate, axis=0, keepdims=True)
        first = jnp.min(jnp.where(gate == mx, rowf, float(LANES)), axis=0, keepdims=True)
        hit = (rowf == first) & (mx > -jnp.inf)
        sel = jnp.where(hit, 1.0, sel)
        gate = jnp.where(rowf == first, -jnp.inf, gate)
    bias_t = jnp.where((sel > 0.0) | (rowf == own_blk) | (rowf >= float(nb)), 0.0, NEG_BIG)
    if nbp < LANES:
        bias_t = jnp.concatenate([bias_t, jnp.zeros((LANES - nbp, tq), jnp.float32)], axis=0)
    q_aug = jnp.concatenate([q, bias_t.T.astype(jnp.bfloat16)], axis=1)

    def qk(c, sbuf):
        kc = kaug[pl.ds(pl.multiple_of(c * ck, ck), ck), :]
        sbuf[...] = lax.dot_general(q_aug, kc, NT_DIMS, preferred_element_type=jnp.float32)

    def softmax_pv(c, sbuf, m_old, causal):
        s = sbuf[...]
        if causal:
            s = s + cmask[t % (ck // tq)]
        m_new = jnp.maximum(m_old, jnp.max(s, axis=1, keepdims=True))
        p = jnp.exp2(s - m_new).astype(vaug.dtype)
        vc = vaug[pl.ds(pl.multiple_of(c * ck, ck), ck), :]
        acc[...] = jnp.exp2(m_old - m_new) * acc[...] + jnp.dot(
            p, vc, preferred_element_type=jnp.float32)
        return m_new

    nc = (t * tq + tq + ck - 1) // ck
    acc[...] = jnp.zeros_like(acc)
    qk(nc - 1, s_a)
    qk(0, s_b)
    m_run = softmax_pv(nc - 1, s_a, jnp.full((tq, 1), NEG_BIG, jnp.float32), True)

    def pair(u, m_run):
        c = 2 * u
        qk(c + 1, s_a)
        m_run = softmax_pv(c, s_b, m_run, False)
        qk(jnp.minimum(c + 2, n_chunks - 1), s_b)
        return softmax_pv(c + 1, s_a, m_run, False)

    m_run = lax.fori_loop(0, (nc - 1) // 2, pair, m_run)

    @pl.when((nc - 1) % 2 == 1)
    def _():
        softmax_pv(nc - 2, s_b, m_run, False)

    o_ref[0] = (acc[:, 0:d] / acc[:, d:2 * d]).astype(o_ref.dtype)


def moba_attention(qkv, n_heads):
    b, s, _ = qkv.shape
    d = HEAD_DIM
    blk = MOBA_BLOCK
    assert s % blk == 0 and blk & (blk - 1) == 0
    nb = s // blk
    assert nb <= LANES
    ck = _pick(s, 4 * blk)
    qb = math.gcd(nb, ck // blk)
    tq = qb * blk
    return pl.pallas_call(
        functools.partial(_moba_kernel, nb=nb, ck=ck, qb=qb),
        grid=(b, n_heads, nb // qb),
        in_specs=[
            pl.BlockSpec((1, tq, d), lambda bi, h, i: (bi, i, h)),
            pl.BlockSpec((1, s, d), lambda bi, h, i: (bi, 0, n_heads + h)),
            pl.BlockSpec((1, s, d), lambda bi, h, i: (bi, 0, 2 * n_heads + h)),
        ],
        out_specs=pl.BlockSpec((1, tq, d), lambda bi, h, i: (bi, i, h)),
        out_shape=jax.ShapeDtypeStruct((b, s, n_heads * d), jnp.bfloat16),
        scratch_shapes=[pltpu.VMEM((s, d + LANES), jnp.bfloat16),
                        pltpu.VMEM((s, 2 * d), jnp.bfloat16),
                        pltpu.VMEM((2 * LANES, d), jnp.bfloat16),
                        pltpu.VMEM((ck // tq, tq, ck), jnp.float32),
                        pltpu.VMEM((tq, 2 * d), jnp.float32),
                        pltpu.VMEM((tq, ck), jnp.float32),
                        pltpu.VMEM((tq, ck), jnp.float32)],
        compiler_params=_cparams("parallel", "parallel", "arbitrary"),
        name="moba_attention",
    )(qkv, qkv, qkv)


def _proj_resid_kernel(a_ref, w_ref, r_ref, o_ref, wbf):
    _cast_weight_tile(w_ref, wbf)
    tm = o_ref.shape[0]
    rc = min(tm, EPILOGUE_ROWS)
    for r in range(tm // rc):
        rows = pl.ds(r * rc, rc)
        o_ref[rows, :] = DEEPNORM_ALPHA * r_ref[rows, :] + jnp.dot(
            a_ref[rows, :], wbf[...], preferred_element_type=jnp.float32)


def proj_resid(a_bf, w, layer, resid):
    m, k = a_bf.shape
    n = w.shape[2]
    tm = _pick(m, 1024)
    tn = _pick(n, 512)
    return pl.pallas_call(
        _proj_resid_kernel,
        grid=(n // tn, m // tm),
        in_specs=[
            pl.BlockSpec((tm, k), lambda j, i: (i, 0)),
            pl.BlockSpec((1, k, tn), lambda j, i: (layer, 0, j)),
            pl.BlockSpec((tm, tn), lambda j, i: (i, j)),
        ],
        out_specs=pl.BlockSpec((tm, tn), lambda j, i: (i, j)),
        out_shape=jax.ShapeDtypeStruct((m, n), jnp.float32),
        scratch_shapes=[pltpu.VMEM((k, tn), jnp.bfloat16)],
        compiler_params=_cparams("parallel", "arbitrary"),
        name="proj_resid",
    )(a_bf, w, resid)


def _layer_norm_rows(z, g, b):
    mu = jnp.mean(z, axis=1, keepdims=True)
    zc = z - mu
    var = jnp.mean(zc * zc, axis=1, keepdims=True)
    return zc * lax.rsqrt(var + LN_EPS) * g + b


def _ln_router_kernel(z_ref, g_ref, b_ref, rw_ref, rb_ref,
                      y_ref, idx_ref, wt_ref, rank_ref, cnt_ref, carry):
    t = pl.program_id(0)

    @pl.when(t == 0)
    def _():
        carry[...] = jnp.zeros_like(carry)

    y = _layer_norm_rows(z_ref[...], g_ref[...], b_ref[...])
    y_ref[...] = y
    logits = jnp.dot(y, rw_ref[...], preferred_element_type=jnp.float32,
                     precision=lax.Precision.HIGHEST) + rb_ref[...]
    tm = logits.shape[0]
    lanef = lax.broadcasted_iota(jnp.int32, logits.shape, 1).astype(jnp.float32)
    vals, idxs = [], []
    onehot = jnp.zeros(logits.shape, jnp.float32)
    for _ in range(TOP_K):
        mx = jnp.max(logits, axis=1, keepdims=True)
        first = jnp.min(jnp.where(logits == mx, lanef, float(LANES)), axis=1, keepdims=True)
        hit = lanef == first
        onehot = jnp.where(hit, 1.0, onehot)
        logits = jnp.where(hit, -jnp.inf, logits)
        vals.append(mx)
        idxs.append(first)
    es = [jnp.exp(v - vals[0]) for v in vals]
    den = es[0] + es[1] + es[2] + es[3]

    r = lax.broadcasted_iota(jnp.int32, (tm, tm), 0)
    c = lax.broadcasted_iota(jnp.int32, (tm, tm), 1)
    tri = jnp.where(c < r, 1.0, 0.0).astype(jnp.bfloat16)
    before = jnp.dot(tri, onehot.astype(jnp.bfloat16),
                     preferred_element_type=jnp.float32) + carry[...]
    for k in range(TOP_K):
        rk = jnp.sum(jnp.where(lanef == idxs[k], before, 0.0), axis=1, keepdims=True)
        rank_ref[:, k:k + 1] = rk.astype(jnp.int32)
        idx_ref[:, k:k + 1] = idxs[k].astype(jnp.int32)
        wt_ref[:, k:k + 1] = es[k] / den
    carry[...] = carry[...] + jnp.sum(onehot, axis=0, keepdims=True)
    cnt_ref[...] = carry[...].astype(jnp.int32)


def ln_router(z, g, b, router_w, router_b):
    n, dm = z.shape
    e = router_w.shape[1]
    assert TOP_K <= e <= LANES
    rw = jnp.zeros((dm, LANES), jnp.float32).at[:, :e].set(router_w.astype(jnp.float32))
    rb = jnp.full((1, LANES), NEG_BIG, jnp.float32).at[0, :e].set(router_b.astype(jnp.float32))
    tm = _pick(n, 256)
    row = lambda w: pl.BlockSpec((tm, w), lambda t: (t, 0))
    const = lambda r, w: pl.BlockSpec((r, w), lambda t: (0, 0))
    return pl.pallas_call(
        _ln_router_kernel,
        grid=(n // tm,),
        in_specs=[row(dm), const(1, dm), const(1, dm), const(dm, LANES), const(1, LANES)],
        out_specs=[row(dm), row(TOP_K), row(TOP_K), row(TOP_K), const(1, LANES)],
        out_shape=[
            jax.ShapeDtypeStruct((n, dm), jnp.float32),
            jax.ShapeDtypeStruct((n, TOP_K), jnp.int32),
            jax.ShapeDtypeStruct((n, TOP_K), jnp.float32),
            jax.ShapeDtypeStruct((n, TOP_K), jnp.int32),
            jax.ShapeDtypeStruct((1, LANES), jnp.int32),
        ],
        scratch_shapes=[pltpu.VMEM((1, LANES), jnp.float32)],
        compiler_params=_cparams("arbitrary"),
        name="ln_router",
    )(z, g.reshape(1, dm), b.reshape(1, dm), rw, rb)


def _expert_up_kernel(be_ref, nu_ref, nx_ref, tok_ref, x_hbm, wg_hbm, bg_ref, wu_hbm, bu_ref,
                      h_ref, xbuf, wstage, wg_bf, wu_bf, sem, wsem, *, tm, layer):
    t = pl.program_id(0)
    n_used = nu_ref[0]
    slot = t % 2

    def weight_copies(e):
        return (pltpu.make_async_copy(wg_hbm.at[layer, e], wstage.at[0], wsem.at[0]),
                pltpu.make_async_copy(wu_hbm.at[layer, e], wstage.at[1], wsem.at[1]))

    @pl.when((t == 0) & (n_used > 0))
    def _():
        for cp in weight_copies(be_ref[0]):
            cp.start()

    @pl.when((t < n_used) & ((t == 0) | (be_ref[t] != be_ref[jnp.maximum(t - 1, 0)])))
    def _():
        for cp in weight_copies(be_ref[t]):
            cp.wait()
        rows_per_cast = math.gcd(wg_bf.shape[0], 512)

        def cast_rows(r, _):
            rows = pl.ds(pl.multiple_of(r * rows_per_cast, rows_per_cast), rows_per_cast)
            wg_bf[rows, :] = wstage[0, rows, :].astype(wg_bf.dtype)
            wu_bf[rows, :] = wstage[1, rows, :].astype(wu_bf.dtype)
            return 0

        lax.fori_loop(0, wg_bf.shape[0] // rows_per_cast, cast_rows, 0)

        @pl.when(nx_ref[t] >= 0)
        def _():
            for cp in weight_copies(nx_ref[t]):
                cp.start()

    def row_copy(blk, r, sl):
        return pltpu.make_async_copy(x_hbm.at[pl.ds(tok_ref[blk * tm + r], 1)],
                                     xbuf.at[sl, pl.ds(r, 1)], sem.at[sl])

    def start_block(blk, sl):
        def body(r, _):
            row_copy(blk, r, sl).start()
            return 0
        lax.fori_loop(0, tm, body, 0, unroll=8)

    def wait_block(blk, sl):
        def body(r, _):
            row_copy(blk, r, sl).wait()
            return 0
        lax.fori_loop(0, tm, body, 0, unroll=8)

    @pl.when((t == 0) & (n_used > 0))
    def _():
        start_block(0, 0)

    @pl.when(t < n_used)
    def _():
        wait_block(t, slot)
        x = xbuf[slot].astype(jnp.bfloat16)
        nxt = jnp.minimum(t + 1, n_used - 1)
        for r in range(tm):
            row_copy(nxt, r, 1 - slot).start()
        g = jnp.dot(x, wg_bf[...], preferred_element_type=jnp.float32) + bg_ref[0]
        u = jnp.dot(x, wu_bf[...], preferred_element_type=jnp.float32) + bu_ref[0]
        g = jnp.minimum(g, SWIGLU_LIMIT)
        u = jnp.clip(u, -SWIGLU_LIMIT, SWIGLU_LIMIT)
        h = g * jax.nn.sigmoid(SWIGLU_ALPHA * g) * (u + 1.0)
        h_ref[...] = h.astype(h_ref.dtype)

    @pl.when(t + 1 == n_used)
    def _():
        wait_block(t, 1 - slot)

    @pl.when(t >= n_used)
    def _():
        h_ref[...] = jnp.zeros_like(h_ref)


def _expert_down_kernel(be_ref, nu_ref, h_ref, wd_ref, bd_ref, y_ref, wbf):
    t = pl.program_id(0)

    @pl.when((t == 0) | (be_ref[t] != be_ref[jnp.maximum(t - 1, 0)]))
    def _():
        wbf[...] = wd_ref[0, 0].astype(wbf.dtype)

    @pl.when(t < nu_ref[0])
    def _():
        y_ref[...] = jnp.dot(h_ref[...], wbf[...],
                             preferred_element_type=jnp.float32) + bd_ref[0]

    @pl.when(t >= nu_ref[0])
    def _():
        y_ref[...] = jnp.zeros_like(y_ref)


def expert_ffn(x, row_tok, block_exp, n_used, next_exp, wg, bg, wu, bu, wd, layer, bd, tm):
    p = row_tok.shape[0]
    dm = x.shape[1]
    _, e, _, f = wg.shape
    nblk = p // tm
    bias_spec = pl.BlockSpec((1, 1, f), lambda t, be, nu, nx, tok: (be[t], 0, 0))
    h = pl.pallas_call(
        functools.partial(_expert_up_kernel, tm=tm, layer=layer),
        grid_spec=pltpu.PrefetchScalarGridSpec(
            num_scalar_prefetch=4,
            grid=(nblk,),
            in_specs=[
                pl.BlockSpec(memory_space=pl.ANY),
                pl.BlockSpec(memory_space=pl.ANY),
                bias_spec,
                pl.BlockSpec(memory_space=pl.ANY),
                bias_spec,
            ],
            out_specs=pl.BlockSpec((tm, f), lambda t, be, nu, nx, tok: (t, 0)),
            scratch_shapes=[pltpu.VMEM((2, tm, dm), x.dtype),
                            pltpu.VMEM((2, dm, f), jnp.float32),
                            pltpu.VMEM((dm, f), jnp.bfloat16), pltpu.VMEM((dm, f), jnp.bfloat16),
                            pltpu.SemaphoreType.DMA((2,)), pltpu.SemaphoreType.DMA((2,))],
        ),
        out_shape=jax.ShapeDtypeStruct((p, f), jnp.bfloat16),
        compiler_params=_cparams("arbitrary"),
        name="expert_up",
    )(block_exp, n_used, next_exp, row_tok, x, wg, bg.reshape(e, 1, f), wu, bu.reshape(e, 1, f))
    return pl.pallas_call(
        _expert_down_kernel,
        grid_spec=pltpu.PrefetchScalarGridSpec(
            num_scalar_prefetch=2,
            grid=(nblk,),
            in_specs=[
                pl.BlockSpec((tm, f), lambda t, be, nu: (t, 0)),
                pl.BlockSpec((1, 1, f, dm), lambda t, be, nu: (layer, be[t], 0, 0)),
                pl.BlockSpec((1, 1, dm), lambda t, be, nu: (be[t], 0, 0)),
            ],
            out_specs=pl.BlockSpec((tm, dm), lambda t, be, nu: (t, 0)),
            scratch_shapes=[pltpu.VMEM((f, dm), jnp.bfloat16)],
        ),
        out_shape=jax.ShapeDtypeStruct((p, dm), jnp.float32),
        compiler_params=_cparams("arbitrary"),
        name="expert_down",
    )(block_exp, n_used, h, wd, bd.reshape(e, 1, dm))


def _combine_ln_kernel(dest_ref, ys_hbm, wt_ref, x_ref, g_ref, b_ref, o_ref, obf_ref,
                       buf, sem, *, tc):
    t = pl.program_id(0)
    slot = t % 2

    def row_copy(tile, r, k, sl):
        return pltpu.make_async_copy(
            ys_hbm.at[pl.ds(dest_ref[(tile * tc + r) * TOP_K + k], 1)],
            buf.at[sl, k, pl.ds(r, 1)], sem.at[sl])

    def start_tile(tile, sl):
        def body(r, _):
            for k in range(TOP_K):
                row_copy(tile, r, k, sl).start()
            return 0
        lax.fori_loop(0, tc, body, 0, unroll=2)

    def wait_tile(tile, sl):
        def body(r, _):
            for k in range(TOP_K):
                row_copy(tile, r, k, sl).wait()
            return 0
        lax.fori_loop(0, tc, body, 0, unroll=2)

    @pl.when(t == 0)
    def _():
        start_tile(0, 0)

    @pl.when(t + 1 < pl.num_programs(0))
    def _():
        start_tile(t + 1, 1 - slot)

    wait_tile(t, slot)
    z = DEEPNORM_ALPHA * x_ref[...]
    for k in range(TOP_K):
        z = z + wt_ref[:, k:k + 1] * buf[slot, k]
    y = _layer_norm_rows(z, g_ref[...], b_ref[...])
    o_ref[...] = y
    obf_ref[...] = y.astype(obf_ref.dtype)


def combine_ln(ys, dest, wt, x, g, b):
    n, dm = x.shape
    tc = _pick(n, 128)
    row = lambda w: pl.BlockSpec((tc, w), lambda t, d: (t, 0))
    const = lambda w: pl.BlockSpec((1, w), lambda t, d: (0, 0))
    return pl.pallas_call(
        functools.partial(_combine_ln_kernel, tc=tc),
        grid_spec=pltpu.PrefetchScalarGridSpec(
            num_scalar_prefetch=1,
            grid=(n // tc,),
            in_specs=[pl.BlockSpec(memory_space=pl.ANY), row(TOP_K), row(dm), const(dm), const(dm)],
            out_specs=[row(dm), row(dm)],
            scratch_shapes=[pltpu.VMEM((2, TOP_K, tc, dm), jnp.float32),
                            pltpu.SemaphoreType.DMA((2,))],
        ),
        out_shape=[jax.ShapeDtypeStruct((n, dm), jnp.float32),
                   jax.ShapeDtypeStruct((n, dm), jnp.bfloat16)],
        compiler_params=_cparams("arbitrary"),
        name="combine_ln",
    )(dest.reshape(-1), ys, wt, x, g.reshape(1, dm), b.reshape(1, dm))


def _rope_tables(positions):
    inv = 1.0 / (ROPE_THETA ** (jnp.arange(0, ROT_DIM, 2, dtype=jnp.float32) / ROT_DIM))
    ang = positions.astype(jnp.float32).reshape(-1, 1) * inv
    cos, sin = jnp.cos(ang), jnp.sin(ang)
    n = ang.shape[0]
    rest = LANES - ROT_DIM
    c = jnp.concatenate([cos, cos, jnp.ones((n, rest), jnp.float32)], axis=1)
    s = jnp.concatenate([-sin, sin, jnp.zeros((n, rest), jnp.float32)], axis=1)
    return c, s


def moe_ln(z, ln_g, ln_b, router_w, router_b, wg, bg, wu, bu, wd, layer, bd, ln2_g, ln2_b,
           tm=256):
    n, dm = z.shape
    e = router_w.shape[1]
    x1, idx, wt, rank, cnt = ln_router(z, ln_g, ln_b, router_w, router_b)
    counts = cnt[0, :e]
    padded = (counts + tm - 1) // tm * tm
    pend = jnp.cumsum(padded)
    pstart = pend - padded
    dest = pstart[idx] + rank
    nblk = (n * TOP_K) // tm + e
    tok = jnp.broadcast_to(jnp.arange(n, dtype=jnp.int32)[:, None], (n, TOP_K))
    row_tok = jnp.zeros((nblk * tm,), jnp.int32).at[dest.reshape(-1)].set(tok.reshape(-1))
    starts = jnp.arange(nblk, dtype=jnp.int32) * tm
    block_exp = jnp.minimum(jnp.sum(pend[None, :] <= starts[:, None], axis=1),
                            e - 1).astype(jnp.int32)
    n_used = (pend[-1:] // tm).astype(jnp.int32)
    ids = jnp.arange(e, dtype=jnp.int32)
    later_used = (ids[None, :] > ids[:, None]) & (counts[None, :] > 0)
    next_of = jnp.min(jnp.where(later_used, ids[None, :], e), axis=1)
    next_exp = jnp.where(next_of < e, next_of, -1)[block_exp].astype(jnp.int32)
    ys = expert_ffn(x1, row_tok, block_exp, n_used, next_exp, wg, bg, wu, bu, wd, layer, bd, tm)
    return combine_ln(ys, dest.astype(jnp.int32), wt, x1, ln2_g, ln2_b)


def kernel(x, positions, diff_w_qkv, diff_lambda_q1, diff_lambda_k1, diff_lambda_q2, diff_lambda_k2, diff_subln_g, diff_w_o, moba_w_qkv, moba_w_o, ln_mix_g, ln_mix_b, router_w, router_b, exp_w_gate, exp_b_gate, exp_w_up, exp_b_up, exp_w_down, exp_b_down, ln_ffn_g, ln_ffn_b):
    b, s, dm = x.shape
    n = b * s
    bf = jnp.bfloat16
    rope_c, rope_s = _rope_tables(positions)
    xf = x.reshape(n, dm)
    x_bf = xf.astype(bf)
    n_mixers = 2
    for i in range(DEPTH):
        j = i // n_mixers
        if i % n_mixers == 0:
            heads = dm // DIFF_V_DIM
            lambda_init = 0.8 - 0.6 * math.exp(-0.3 * i)
            qkv = qkv_proj(x_bf, diff_w_qkv, j, rope_c, rope_s, heads * DIFF_V_DIM)
            att = diff_attention(qkv.reshape(b, s, -1), diff_lambda_q1[j], diff_lambda_k1[j],
                                 diff_lambda_q2[j], diff_lambda_k2[j], diff_subln_g[j],
                                 heads, lambda_init)
            w_o = diff_w_o
        else:
            heads = dm // HEAD_DIM
            qkv = qkv_proj(x_bf, moba_w_qkv, j, rope_c, rope_s, heads * HEAD_DIM)
            att = moba_attention(qkv.reshape(b, s, -1), heads)
            w_o = moba_w_o
        z = proj_resid(att.reshape(n, -1), w_o, j, xf)
        xf, x_bf = moe_ln(z, ln_mix_g[i], ln_mix_b[i], router_w[i], router_b[i],
                          exp_w_gate, exp_b_gate[i], exp_w_up, exp_b_up[i], exp_w_down, i,
                          exp_b_down[i],
                          ln_ffn_g[i], ln_ffn_b[i])
    return xf.reshape(b, s, dm)
```

```python
import functools
import math

import jax
import jax.numpy as jnp
from jax import lax
from jax.experimental import pallas as pl
from jax.experimental.pallas import tpu as pltpu

DEPTH = 2
ROPE_THETA = 500000.0
HEAD_DIM = 128
ROT_DIM = HEAD_DIM // 4
ROT_HALF = ROT_DIM // 2
DIFF_V_DIM = 2 * HEAD_DIM
MOBA_BLOCK = 256
MOBA_TOPK = 3
TOP_K = 4
SWIGLU_LIMIT = 7.0
SWIGLU_ALPHA = 1.702
LN_EPS = 1e-5
RMS_EPS = 1e-5
DEEPNORM_ALPHA = (2 * DEPTH) ** 0.25

LANES = 128
QK_FOLD = HEAD_DIM ** -0.5 * math.log2(math.e)
NEG_BIG = -1e30
NT_DIMS = (((1,), (1,)), ((), ()))
EPILOGUE_ROWS = 256
VMEM_LIMIT = 56 * 1024 * 1024


def _cparams(*sem):
    return pltpu.CompilerParams(dimension_semantics=sem, vmem_limit_bytes=VMEM_LIMIT)


def _pick(n, pref):
    t = min(pref, n)
    while n % t:
        t //= 2
    return t


def _cast_weight_tile(w_ref, wbf):
    @pl.when(pl.program_id(1) == 0)
    def _():
        wbf[...] = w_ref[0].astype(wbf.dtype)


def _qkv_kernel(x_ref, w_ref, c_ref, s_ref, o_ref, wbf, *, n_q_tiles, n_rope_tiles):
    j = pl.program_id(0)
    _cast_weight_tile(w_ref, wbf)
    tm, tn = o_ref.shape
    rc = min(tm, EPILOGUE_ROWS)

    def row_chunks():
        for r in range(tm // rc):
            rows = pl.ds(r * rc, rc)
            yield rows, jnp.dot(x_ref[rows, :], wbf[...], preferred_element_type=jnp.float32)

    @pl.when(j < n_rope_tiles)
    def _():
        fold = jnp.where(j < n_q_tiles, QK_FOLD, 1.0)
        lane = lax.broadcasted_iota(jnp.int32, (rc, LANES), 1)
        for rows, acc in row_chunks():
            c = c_ref[rows, :] * fold
            s = s_ref[rows, :] * fold
            for h in range(tn // LANES):
                a = acc[:, h * LANES:(h + 1) * LANES]
                partner = jnp.where(lane < ROT_HALF,
                                    pltpu.roll(a, LANES - ROT_HALF, 1),
                                    pltpu.roll(a, ROT_HALF, 1))
                o_ref[rows, h * LANES:(h + 1) * LANES] = (a * c + partner * s).astype(o_ref.dtype)

    @pl.when(j >= n_rope_tiles)
    def _():
        for rows, acc in row_chunks():
            o_ref[rows, :] = acc.astype(o_ref.dtype)


def qkv_proj(x_bf, w, layer, rope_c, rope_s, n_q_cols):
    m, k = x_bf.shape
    n = w.shape[2]
    tm = _pick(m, 1024)
    tn = _pick(math.gcd(n, n_q_cols), 512)
    return pl.pallas_call(
        functools.partial(_qkv_kernel, n_q_tiles=n_q_cols // tn, n_rope_tiles=2 * n_q_cols // tn),
        grid=(n // tn, m // tm),
        in_specs=[
            pl.BlockSpec((tm, k), lambda j, i: (i, 0)),
            pl.BlockSpec((1, k, tn), lambda j, i: (layer, 0, j)),
            pl.BlockSpec((tm, LANES), lambda j, i: (i, 0)),
            pl.BlockSpec((tm, LANES), lambda j, i: (i, 0)),
        ],
        out_specs=pl.BlockSpec((tm, tn), lambda j, i: (i, j)),
        out_shape=jax.ShapeDtypeStruct((m, n), jnp.bfloat16),
        scratch_shapes=[pltpu.VMEM((k, tn), jnp.bfloat16)],
        compiler_params=_cparams("parallel", "arbitrary"),
        name="qkv_proj",
    )(x_bf, w, rope_c, rope_s)


def _diff_attn_kernel(lq1_ref, lk1_ref, lq2_ref, lk2_ref, g_ref, q_ref, k_ref, v_ref,
                      o_ref, acc1, acc2, s_a, s_b, *, tq, lambda_init):
    i = pl.program_id(2)
    d = HEAD_DIM
    q = q_ref[0]
    qs = (q[:, :d], q[:, d:])
    accs = (acc1, acc2)

    def qk(j, sbuf):
        kj = k_ref[0, pl.ds(pl.multiple_of(j * tq, tq), tq), :]
        for c in range(2):
            sbuf[c] = lax.dot_general(qs[c], kj[:, c * d:(c + 1) * d], NT_DIMS,
                                      preferred_element_type=jnp.float32)

    def softmax_pv(j, sbuf, carry, masked):
        vj = v_ref[0, pl.ds(pl.multiple_of(j * tq, tq), tq), :]
        new = []
        for c in range(2):
            m_old, l_old = carry[2 * c], carry[2 * c + 1]
            s = sbuf[c]
            if masked:
                row = lax.broadcasted_iota(jnp.int32, s.shape, 0)
                col = lax.broadcasted_iota(jnp.int32, s.shape, 1)
                s = jnp.where(col <= row, s, NEG_BIG)
            m_new = jnp.maximum(m_old, jnp.max(s, axis=1, keepdims=True))
            alpha = jnp.exp2(m_old - m_new)
            p = jnp.exp2(s - m_new)
            l_new = alpha * l_old + jnp.sum(p, axis=1, keepdims=True)
            accs[c][...] = alpha * accs[c][...] + jnp.dot(
                p.astype(vj.dtype), vj, preferred_element_type=jnp.float32)
            new += [m_new, l_new]
        return tuple(new)

    acc1[...] = jnp.zeros_like(acc1)
    acc2[...] = jnp.zeros_like(acc2)
    m0 = jnp.full((tq, 1), NEG_BIG, jnp.float32)
    l0 = jnp.zeros((tq, 1), jnp.float32)

    qk(0, s_a)

    def pair(t, carry):
        j = 2 * t
        qk(j + 1, s_b)
        carry = softmax_pv(j, s_a, carry, False)
        qk(j + 2, s_a)
        return softmax_pv(j + 1, s_b, carry, False)

    carry = lax.fori_loop(0, i // 2, pair, (m0, l0, m0, l0))

    def last_even(carry):
        return softmax_pv(i, s_a, carry, True)

    def last_odd(carry):
        qk(i, s_b)
        carry = softmax_pv(i - 1, s_a, carry, False)
        return softmax_pv(i, s_b, carry, True)

    m1, l1, m2, l2 = lax.cond(i % 2 == 0, last_even, last_odd, carry)

    lam = (jnp.exp(jnp.sum(lq1_ref[...] * lk1_ref[...], axis=1, keepdims=True))
           - jnp.exp(jnp.sum(lq2_ref[...] * lk2_ref[...], axis=1, keepdims=True))
           + lambda_init)
    o = acc1[...] / l1 - lam * (acc2[...] / l2)
    o = o * lax.rsqrt(jnp.mean(o * o, axis=1, keepdims=True) + RMS_EPS)
    o = o * g_ref[...] * (1.0 - lambda_init)
    o_ref[0] = o.astype(o_ref.dtype)


def diff_attention(qkv, lq1, lk1, lq2, lk2, subln_g, n_heads, lambda_init):
    b, s, _ = qkv.shape
    dv = DIFF_V_DIM
    tq = _pick(s, 512)
    vec = lambda a: a.reshape(1, -1).astype(jnp.float32)
    small = lambda n: pl.BlockSpec((1, n), lambda bi, h, i: (0, 0))
    return pl.pallas_call(
        functools.partial(_diff_attn_kernel, tq=tq, lambda_init=lambda_init),
        grid=(b, n_heads, s // tq),
        in_specs=[
            small(HEAD_DIM), small(HEAD_DIM), small(HEAD_DIM), small(HEAD_DIM), small(dv),
            pl.BlockSpec((1, tq, dv), lambda bi, h, i: (bi, i, h)),
            pl.BlockSpec((1, s, dv), lambda bi, h, i: (bi, 0, n_heads + h)),
            pl.BlockSpec((1, s, dv), lambda bi, h, i: (bi, 0, 2 * n_heads + h)),
        ],
        out_specs=pl.BlockSpec((1, tq, dv), lambda bi, h, i: (bi, i, h)),
        out_shape=jax.ShapeDtypeStruct((b, s, n_heads * dv), jnp.bfloat16),
        scratch_shapes=[pltpu.VMEM((tq, dv), jnp.float32), pltpu.VMEM((tq, dv), jnp.float32),
                        pltpu.VMEM((2, tq, tq), jnp.float32), pltpu.VMEM((2, tq, tq), jnp.float32)],
        compiler_params=_cparams("parallel", "parallel", "arbitrary"),
        name="diff_attention",
    )(vec(lq1), vec(lk1), vec(lq2), vec(lk2), vec(subln_g), qkv, qkv, qkv)


def _moba_kernel(q_ref, k_ref, v_ref, o_ref, kaug, vaug, kmean, cmask, acc, s_a, s_b,
                 *, nb, ck, qb):
    t = pl.program_id(2)
    blk = MOBA_BLOCK
    d = HEAD_DIM
    tq = qb * blk
    s_len = nb * blk
    n_chunks = s_len // ck
    shift = blk.bit_length() - 1

    @pl.when(t == 0)
    def _():
        vaug[:, 0:d] = v_ref[0]
        vaug[:, d:2 * d] = jnp.ones((s_len, d), vaug.dtype)
        row = lax.broadcasted_iota(jnp.int32, (tq, ck), 0)
        col = lax.broadcasted_iota(jnp.int32, (tq, ck), 1)
        ahead = (col & (blk - 1)) > (row & (blk - 1))
        for pos in range(ck // tq):
            same_blk = (col >> shift) == (pos * qb + (row >> shift))
            cmask[pos] = jnp.where(same_blk & ahead, NEG_BIG, 0.0)
        r = lax.broadcasted_iota(jnp.int32, (LANES, s_len), 0)
        c = lax.broadcasted_iota(jnp.int32, (LANES, s_len), 1)
        ind = jnp.where((c >> shift) == r, 1.0, 0.0).astype(jnp.bfloat16)
        km = jnp.dot(ind, k_ref[0], preferred_element_type=jnp.float32) * (1.0 / blk)
        hi = km.astype(jnp.bfloat16)
        kmean[0:LANES, :] = hi
        kmean[LANES:2 * LANES, :] = (km - hi.astype(jnp.float32)).astype(jnp.bfloat16)
        rr = lax.broadcasted_iota(jnp.int32, (s_len, LANES), 0)
        cc = lax.broadcasted_iota(jnp.int32, (s_len, LANES), 1)
        kaug[:, 0:d] = k_ref[0]
        kaug[:, d:d + LANES] = jnp.where((rr >> shift) == cc, 1.0, 0.0).astype(jnp.bfloat16)

    q = q_ref[0]
    nbp = -(-nb // 8) * 8
    g2 = lax.dot_general(kmean[...], q, NT_DIMS, preferred_element_type=jnp.float32)
    gate = g2[0:nbp] + g2[LANES:LANES + nbp]
    rowf = lax.broadcasted_iota(jnp.int32, gate.shape, 0).astype(jnp.float32)
    own_blk = (t * qb + (lax.broadcasted_iota(jnp.int32, (1, tq), 1) >> shift)).astype(jnp.float32)
    gate = jnp.where(rowf < own_blk, gate, -jnp.inf)
    sel = jnp.zeros(gate.shape, jnp.float32)
    for _ in range(MOBA_TOPK):
        mx = jnp.max(gate, axis=0, keepdims=True)
        first = jnp.min(jnp.where(gate == mx, rowf, float(LANES)), axis=0, keepdims=True)
        hit = (rowf == first) & (mx > -jnp.inf)
        sel = jnp.where(hit, 1.0, sel)
        gate = jnp.where(rowf == first, -jnp.inf, gate)
    bias_t = jnp.where((sel > 0.0) | (rowf == own_blk) | (rowf >= float(nb)), 0.0, NEG_BIG)
    if nbp < LANES:
        bias_t = jnp.concatenate([bias_t, jnp.zeros((LANES - nbp, tq), jnp.float32)], axis=0)
    q_aug = jnp.concatenate([q, bias_t.T.astype(jnp.bfloat16)], axis=1)

    def qk(c, sbuf):
        kc = kaug[pl.ds(pl.multiple_of(c * ck, ck), ck), :]
        sbuf[...] = lax.dot_general(q_aug, kc, NT_DIMS, preferred_element_type=jnp.float32)

    def softmax_pv(c, sbuf, m_old, causal):
        s = sbuf[...]
        if causal:
            s = s + cmask[t % (ck // tq)]
        m_new = jnp.maximum(m_old, jnp.max(s, axis=1, keepdims=True))
        p = jnp.exp2(s - m_new).astype(vaug.dtype)
        vc = vaug[pl.ds(pl.multiple_of(c * ck, ck), ck), :]
        acc[...] = jnp.exp2(m_old - m_new) * acc[...] + jnp.dot(
            p, vc, preferred_element_type=jnp.float32)
        return m_new

    nc = (t * tq + tq + ck - 1) // ck
    acc[...] = jnp.zeros_like(acc)
    qk(nc - 1, s_a)
    qk(0, s_b)
    m_run = softmax_pv(nc - 1, s_a, jnp.full((tq, 1), NEG_BIG, jnp.float32), True)

    def pair(u, m_run):
        c = 2 * u
        qk(c + 1, s_a)
        m_run = softmax_pv(c, s_b, m_run, False)
        qk(jnp.minimum(c + 2, n_chunks - 1), s_b)
        return softmax_pv(c + 1, s_a, m_run, False)

    m_run = lax.fori_loop(0, (nc - 1) // 2, pair, m_run)

    @pl.when((nc - 1) % 2 == 1)
    def _():
        softmax_pv(nc - 2, s_b, m_run, False)

    o_ref[0] = (acc[:, 0:d] / acc[:, d:2 * d]).astype(o_ref.dtype)


def moba_attention(qkv, n_heads):
    b, s, _ = qkv.shape
    d = HEAD_DIM
    blk = MOBA_BLOCK
    assert s % blk == 0 and blk & (blk - 1) == 0
    nb = s // blk
    assert nb <= LANES
    ck = _pick(s, 4 * blk)
    qb = math.gcd(nb, ck // blk)
    tq = qb * blk
    return pl.pallas_call(
        functools.partial(_moba_kernel, nb=nb, ck=ck, qb=qb),
        grid=(b, n_heads, nb // qb),
        in_specs=[
            pl.BlockSpec((1, tq, d), lambda bi, h, i: (bi, i, h)),
            pl.BlockSpec((1, s, d), lambda bi, h, i: (bi, 0, n_heads + h)),
            pl.BlockSpec((1, s, d), lambda bi, h, i: (bi, 0, 2 * n_heads + h)),
        ],
        out_specs=pl.BlockSpec((1, tq, d), lambda bi, h, i: (bi, i, h)),
        out_shape=jax.ShapeDtypeStruct((b, s, n_heads * d), jnp.bfloat16),
        scratch_shapes=[pltpu.VMEM((s, d + LANES), jnp.bfloat16),
                        pltpu.VMEM((s, 2 * d), jnp.bfloat16),
                        pltpu.VMEM((2 * LANES, d), jnp.bfloat16),
                        pltpu.VMEM((ck // tq, tq, ck), jnp.float32),
                        pltpu.VMEM((tq, 2 * d), jnp.float32),
                        pltpu.VMEM((tq, ck), jnp.float32),
                        pltpu.VMEM((tq, ck), jnp.float32)],
        compiler_params=_cparams("parallel", "parallel", "arbitrary"),
        name="moba_attention",
    )(qkv, qkv, qkv)


def _proj_resid_kernel(a_ref, w_ref, r_ref, o_ref, wbf):
    _cast_weight_tile(w_ref, wbf)
    tm = o_ref.shape[0]
    rc = min(tm, EPILOGUE_ROWS)
    for r in range(tm // rc):
        rows = pl.ds(r * rc, rc)
        o_ref[rows, :] = DEEPNORM_ALPHA * r_ref[rows, :] + jnp.dot(
            a_ref[rows, :], wbf[...], preferred_element_type=jnp.float32)


def proj_resid(a_bf, w, layer, resid):
    m, k = a_bf.shape
    n = w.shape[2]
    tm = _pick(m, 1024)
    tn = _pick(n, 512)
    return pl.pallas_call(
        _proj_resid_kernel,
        grid=(n // tn, m // tm),
        in_specs=[
            pl.BlockSpec((tm, k), lambda j, i: (i, 0)),
            pl.BlockSpec((1, k, tn), lambda j, i: (layer, 0, j)),
            pl.BlockSpec((tm, tn), lambda j, i: (i, j)),
        ],
        out_specs=pl.BlockSpec((tm, tn), lambda j, i: (i, j)),
        out_shape=jax.ShapeDtypeStruct((m, n), jnp.float32),
        scratch_shapes=[pltpu.VMEM((k, tn), jnp.bfloat16)],
        compiler_params=_cparams("parallel", "arbitrary"),
        name="proj_resid",
    )(a_bf, w, resid)


def _layer_norm_rows(z, g, b):
    mu = jnp.mean(z, axis=1, keepdims=True)
    zc = z - mu
    var = jnp.mean(zc * zc, axis=1, keepdims=True)
    return zc * lax.rsqrt(var + LN_EPS) * g + b


def _ln_router_kernel(z_ref, g_ref, b_ref, rw_ref, rb_ref,
                      y_ref, ypk_ref, idx_ref, wt_ref, rank_ref, cnt_ref, carry):
    t = pl.program_id(0)

    @pl.when(t == 0)
    def _():
        carry[...] = jnp.zeros_like(carry)

    y = _layer_norm_rows(z_ref[...], g_ref[...], b_ref[...])
    y_ref[...] = y
    half = y.shape[1] // 2
    hi = lax.bitcast_convert_type(y[:, :half].astype(jnp.bfloat16).astype(jnp.float32), jnp.uint32)
    lo = lax.bitcast_convert_type(y[:, half:].astype(jnp.bfloat16).astype(jnp.float32), jnp.uint32)
    ypk_ref[...] = (hi & jnp.uint32(0xFFFF0000)) | (lo >> 16)
    logits = jnp.dot(y, rw_ref[...], preferred_element_type=jnp.float32,
                     precision=lax.Precision.HIGHEST) + rb_ref[...]
    tm = logits.shape[0]
    lanef = lax.broadcasted_iota(jnp.int32, logits.shape, 1).astype(jnp.float32)
    vals, idxs = [], []
    onehot = jnp.zeros(logits.shape, jnp.float32)
    for _ in range(TOP_K):
        mx = jnp.max(logits, axis=1, keepdims=True)
        first = jnp.min(jnp.where(logits == mx, lanef, float(LANES)), axis=1, keepdims=True)
        hit = lanef == first
        onehot = jnp.where(hit, 1.0, onehot)
        logits = jnp.where(hit, -jnp.inf, logits)
        vals.append(mx)
        idxs.append(first)
    es = [jnp.exp(v - vals[0]) for v in vals]
    den = es[0] + es[1] + es[2] + es[3]

    r = lax.broadcasted_iota(jnp.int32, (tm, tm), 0)
    c = lax.broadcasted_iota(jnp.int32, (tm, tm), 1)
    tri = jnp.where(c < r, 1.0, 0.0).astype(jnp.bfloat16)
    before = jnp.dot(tri, onehot.astype(jnp.bfloat16),
                     preferred_element_type=jnp.float32) + carry[...]
    for k in range(TOP_K):
        rk = jnp.sum(jnp.where(lanef == idxs[k], before, 0.0), axis=1, keepdims=True)
        rank_ref[:, k:k + 1] = rk.astype(jnp.int32)
        idx_ref[:, k:k + 1] = idxs[k].astype(jnp.int32)
        wt_ref[:, k:k + 1] = es[k] / den
    carry[...] = carry[...] + jnp.sum(onehot, axis=0, keepdims=True)
    cnt_ref[...] = carry[...].astype(jnp.int32)


def ln_router(z, g, b, router_w, router_b):
    n, dm = z.shape
    e = router_w.shape[1]
    assert TOP_K <= e <= LANES
    rw = jnp.zeros((dm, LANES), jnp.float32).at[:, :e].set(router_w.astype(jnp.float32))
    rb = jnp.full((1, LANES), NEG_BIG, jnp.float32).at[0, :e].set(router_b.astype(jnp.float32))
    tm = _pick(n, 256)
    row = lambda w: pl.BlockSpec((tm, w), lambda t: (t, 0))
    const = lambda r, w: pl.BlockSpec((r, w), lambda t: (0, 0))
    return pl.pallas_call(
        _ln_router_kernel,
        grid=(n // tm,),
        in_specs=[row(dm), const(1, dm), const(1, dm), const(dm, LANES), const(1, LANES)],
        out_specs=[row(dm), row(dm // 2), row(TOP_K), row(TOP_K), row(TOP_K), const(1, LANES)],
        out_shape=[
            jax.ShapeDtypeStruct((n, dm), jnp.float32),
            jax.ShapeDtypeStruct((n, dm // 2), jnp.uint32),
            jax.ShapeDtypeStruct((n, TOP_K), jnp.int32),
            jax.ShapeDtypeStruct((n, TOP_K), jnp.float32),
            jax.ShapeDtypeStruct((n, TOP_K), jnp.int32),
            jax.ShapeDtypeStruct((1, LANES), jnp.int32),
        ],
        scratch_shapes=[pltpu.VMEM((1, LANES), jnp.float32)],
        compiler_params=_cparams("arbitrary"),
        name="ln_router",
    )(z, g.reshape(1, dm), b.reshape(1, dm), rw, rb)


def _expert_up_kernel(be_ref, nu_ref, nx_ref, tok_ref, x_hbm, wg_hbm, bg_ref, wu_hbm, bu_ref,
                      h_ref, xbuf, wstage, wg_bf, wu_bf, sem, wsem, *, tm, layer):
    t = pl.program_id(0)
    n_used = nu_ref[0]
    slot = t % 2

    def weight_copies(e):
        return (pltpu.make_async_copy(wg_hbm.at[layer, e], wstage.at[0], wsem.at[0]),
                pltpu.make_async_copy(wu_hbm.at[layer, e], wstage.at[1], wsem.at[1]))

    @pl.when((t == 0) & (n_used > 0))
    def _():
        for cp in weight_copies(be_ref[0]):
            cp.start()

    @pl.when((t < n_used) & ((t == 0) | (be_ref[t] != be_ref[jnp.maximum(t - 1, 0)])))
    def _():
        for cp in weight_copies(be_ref[t]):
            cp.wait()
        rows_per_cast = math.gcd(wg_bf.shape[0], 512)

        def cast_rows(r, _):
            rows = pl.ds(pl.multiple_of(r * rows_per_cast, rows_per_cast), rows_per_cast)
            wg_bf[rows, :] = wstage[0, rows, :].astype(wg_bf.dtype)
            wu_bf[rows, :] = wstage[1, rows, :].astype(wu_bf.dtype)
            return 0

        lax.fori_loop(0, wg_bf.shape[0] // rows_per_cast, cast_rows, 0)

        @pl.when(nx_ref[t] >= 0)
        def _():
            for cp in weight_copies(nx_ref[t]):
                cp.start()

    def row_copy(blk, r, sl):
        return pltpu.make_async_copy(x_hbm.at[pl.ds(tok_ref[blk * tm + r], 1)],
                                     xbuf.at[sl, pl.ds(r, 1)], sem.at[sl])

    def start_block(blk, sl):
        def body(r, _):
            row_copy(blk, r, sl).start()
            return 0
        lax.fori_loop(0, tm, body, 0, unroll=8)

    def wait_block(blk, sl):
        def body(r, _):
            row_copy(blk, r, sl).wait()
            return 0
        lax.fori_loop(0, tm, body, 0, unroll=8)

    @pl.when((t == 0) & (n_used > 0))
    def _():
        start_block(0, 0)

    @pl.when(t < n_used)
    def _():
        wait_block(t, slot)
        w = xbuf[slot]
        x = jnp.concatenate(
            [lax.bitcast_convert_type(w & jnp.uint32(0xFFFF0000), jnp.float32),
             lax.bitcast_convert_type(w << 16, jnp.float32)], axis=1).astype(jnp.bfloat16)
        nxt = jnp.minimum(t + 1, n_used - 1)
        for r in range(tm):
            row_copy(nxt, r, 1 - slot).start()
        g = jnp.dot(x, wg_bf[...], preferred_element_type=jnp.float32) + bg_ref[0]
        u = jnp.dot(x, wu_bf[...], preferred_element_type=jnp.float32) + bu_ref[0]
        g = jnp.minimum(g, SWIGLU_LIMIT)
        u = jnp.clip(u, -SWIGLU_LIMIT, SWIGLU_LIMIT)
        h = g * jax.nn.sigmoid(SWIGLU_ALPHA * g) * (u + 1.0)
        h_ref[...] = h.astype(h_ref.dtype)

    @pl.when(t + 1 == n_used)
    def _():
        wait_block(t, 1 - slot)

    @pl.when(t >= n_used)
    def _():
        h_ref[...] = jnp.zeros_like(h_ref)


def _expert_down_kernel(be_ref, nu_ref, h_ref, wd_ref, bd_ref, y_ref, wbf):
    t = pl.program_id(0)

    @pl.when((t == 0) | (be_ref[t] != be_ref[jnp.maximum(t - 1, 0)]))
    def _():
        wbf[...] = wd_ref[0, 0].astype(wbf.dtype)

    @pl.when(t < nu_ref[0])
    def _():
        y_ref[...] = jnp.dot(h_ref[...], wbf[...],
                             preferred_element_type=jnp.float32) + bd_ref[0]

    @pl.when(t >= nu_ref[0])
    def _():
        y_ref[...] = jnp.zeros_like(y_ref)


def expert_ffn(x, row_tok, block_exp, n_used, next_exp, wg, bg, wu, bu, wd, layer, bd, tm):
    p = row_tok.shape[0]
    _, e, dm, f = wg.shape
    nblk = p // tm
    bias_spec = pl.BlockSpec((1, 1, f), lambda t, be, nu, nx, tok: (be[t], 0, 0))
    h = pl.pallas_call(
        functools.partial(_expert_up_kernel, tm=tm, layer=layer),
        grid_spec=pltpu.PrefetchScalarGridSpec(
            num_scalar_prefetch=4,
            grid=(nblk,),
            in_specs=[
                pl.BlockSpec(memory_space=pl.ANY),
                pl.BlockSpec(memory_space=pl.ANY),
                bias_spec,
                pl.BlockSpec(memory_space=pl.ANY),
                bias_spec,
            ],
            out_specs=pl.BlockSpec((tm, f), lambda t, be, nu, nx, tok: (t, 0)),
            scratch_shapes=[pltpu.VMEM((2, tm, x.shape[1]), x.dtype),
                            pltpu.VMEM((2, dm, f), jnp.float32),
                            pltpu.VMEM((dm, f), jnp.bfloat16), pltpu.VMEM((dm, f), jnp.bfloat16),
                            pltpu.SemaphoreType.DMA((2,)), pltpu.SemaphoreType.DMA((2,))],
        ),
        out_shape=jax.ShapeDtypeStruct((p, f), jnp.bfloat16),
        compiler_params=_cparams("arbitrary"),
        name="expert_up",
    )(block_exp, n_used, next_exp, row_tok, x, wg, bg.reshape(e, 1, f), wu, bu.reshape(e, 1, f))
    return pl.pallas_call(
        _expert_down_kernel,
        grid_spec=pltpu.PrefetchScalarGridSpec(
            num_scalar_prefetch=2,
            grid=(nblk,),
            in_specs=[
                pl.BlockSpec((tm, f), lambda t, be, nu: (t, 0)),
                pl.BlockSpec((1, 1, f, dm), lambda t, be, nu: (layer, be[t], 0, 0)),
                pl.BlockSpec((1, 1, dm), lambda t, be, nu: (be[t], 0, 0)),
            ],
            out_specs=pl.BlockSpec((tm, dm), lambda t, be, nu: (t, 0)),
            scratch_shapes=[pltpu.VMEM((f, dm), jnp.bfloat16)],
        ),
        out_shape=jax.ShapeDtypeStruct((p, dm), jnp.float32),
        compiler_params=_cparams("arbitrary"),
        name="expert_down",
    )(block_exp, n_used, h, wd, bd.reshape(e, 1, dm))


def _combine_ln_kernel(dest_ref, ys_hbm, wt_ref, x_ref, g_ref, b_ref, o_ref, obf_ref,
                       buf, sem, *, tc):
    t = pl.program_id(0)
    slot = t % 2

    def row_copy(tile, r, k, sl):
        return pltpu.make_async_copy(
            ys_hbm.at[pl.ds(dest_ref[(tile * tc + r) * TOP_K + k], 1)],
            buf.at[sl, k, pl.ds(r, 1)], sem.at[sl])

    def start_tile(tile, sl):
        def body(r, _):
            for k in range(TOP_K):
                row_copy(tile, r, k, sl).start()
            return 0
        lax.fori_loop(0, tc, body, 0, unroll=2)

    def wait_tile(tile, sl):
        def body(r, _):
            for k in range(TOP_K):
                row_copy(tile, r, k, sl).wait()
            return 0
        lax.fori_loop(0, tc, body, 0, unroll=2)

    @pl.when(t == 0)
    def _():
        start_tile(0, 0)

    @pl.when(t + 1 < pl.num_programs(0))
    def _():
        start_tile(t + 1, 1 - slot)

    wait_tile(t, slot)
    z = DEEPNORM_ALPHA * x_ref[...]
    for k in range(TOP_K):
        z = z + wt_ref[:, k:k + 1] * buf[slot, k]
    y = _layer_norm_rows(z, g_ref[...], b_ref[...])
    o_ref[...] = y
    obf_ref[...] = y.astype(obf_ref.dtype)


def combine_ln(ys, dest, wt, x, g, b):
    n, dm = x.shape
    tc = _pick(n, 128)
    row = lambda w: pl.BlockSpec((tc, w), lambda t, d: (t, 0))
    const = lambda w: pl.BlockSpec((1, w), lambda t, d: (0, 0))
    return pl.pallas_call(
        functools.partial(_combine_ln_kernel, tc=tc),
        grid_spec=pltpu.PrefetchScalarGridSpec(
            num_scalar_prefetch=1,
            grid=(n // tc,),
            in_specs=[pl.BlockSpec(memory_space=pl.ANY), row(TOP_K), row(dm), const(dm), const(dm)],
            out_specs=[row(dm), row(dm)],
            scratch_shapes=[pltpu.VMEM((2, TOP_K, tc, dm), jnp.float32),
                            pltpu.SemaphoreType.DMA((2,))],
        ),
        out_shape=[jax.ShapeDtypeStruct((n, dm), jnp.float32),
                   jax.ShapeDtypeStruct((n, dm), jnp.bfloat16)],
        compiler_params=_cparams("arbitrary"),
        name="combine_ln",
    )(dest.reshape(-1), ys, wt, x, g.reshape(1, dm), b.reshape(1, dm))


def _rope_tables(positions):
    inv = 1.0 / (ROPE_THETA ** (jnp.arange(0, ROT_DIM, 2, dtype=jnp.float32) / ROT_DIM))
    ang = positions.astype(jnp.float32).reshape(-1, 1) * inv
    cos, sin = jnp.cos(ang), jnp.sin(ang)
    n = ang.shape[0]
    rest = LANES - ROT_DIM
    c = jnp.concatenate([cos, cos, jnp.ones((n, rest), jnp.float32)], axis=1)
    s = jnp.concatenate([-sin, sin, jnp.zeros((n, rest), jnp.float32)], axis=1)
    return c, s


def moe_ln(z, ln_g, ln_b, router_w, router_b, wg, bg, wu, bu, wd, layer, bd, ln2_g, ln2_b,
           tm=256):
    n, dm = z.shape
    e = router_w.shape[1]
    x1, x1_pk, idx, wt, rank, cnt = ln_router(z, ln_g, ln_b, router_w, router_b)
    counts = cnt[0, :e]
    padded = (counts + tm - 1) // tm * tm
    pend = jnp.cumsum(padded)
    pstart = pend - padded
    dest = pstart[idx] + rank
    nblk = (n * TOP_K) // tm + e
    tok = jnp.broadcast_to(jnp.arange(n, dtype=jnp.int32)[:, None], (n, TOP_K))
    row_tok = jnp.zeros((nblk * tm,), jnp.int32).at[dest.reshape(-1)].set(tok.reshape(-1))
    starts = jnp.arange(nblk, dtype=jnp.int32) * tm
    block_exp = jnp.minimum(jnp.sum(pend[None, :] <= starts[:, None], axis=1),
                            e - 1).astype(jnp.int32)
    n_used = (pend[-1:] // tm).astype(jnp.int32)
    ids = jnp.arange(e, dtype=jnp.int32)
    later_used = (ids[None, :] > ids[:, None]) & (counts[None, :] > 0)
    next_of = jnp.min(jnp.where(later_used, ids[None, :], e), axis=1)
    next_exp = jnp.where(next_of < e, next_of, -1)[block_exp].astype(jnp.int32)
    ys = expert_ffn(x1_pk, row_tok, block_exp, n_used, next_exp, wg, bg, wu, bu, wd, layer, bd, tm)
    return combine_ln(ys, dest.astype(jnp.int32), wt, x1, ln2_g, ln2_b)


def kernel(x, positions, diff_w_qkv, diff_lambda_q1, diff_lambda_k1, diff_lambda_q2, diff_lambda_k2, diff_subln_g, diff_w_o, moba_w_qkv, moba_w_o, ln_mix_g, ln_mix_b, router_w, router_b, exp_w_gate, exp_b_gate, exp_w_up, exp_b_up, exp_w_down, exp_b_down, ln_ffn_g, ln_ffn_b):
    b, s, dm = x.shape
    n = b * s
    bf = jnp.bfloat16
    rope_c, rope_s = _rope_tables(positions)
    xf = x.reshape(n, dm)
    x_bf = xf.astype(bf)
    n_mixers = 2
    for i in range(DEPTH):
        j = i // n_mixers
        if i % n_mixers == 0:
            heads = dm // DIFF_V_DIM
            lambda_init = 0.8 - 0.6 * math.exp(-0.3 * i)
            qkv = qkv_proj(x_bf, diff_w_qkv, j, rope_c, rope_s, heads * DIFF_V_DIM)
            att = diff_attention(qkv.reshape(b, s, -1), diff_lambda_q1[j], diff_lambda_k1[j],
                                 diff_lambda_q2[j], diff_lambda_k2[j], diff_subln_g[j],
                                 heads, lambda_init)
            w_o = diff_w_o
        else:
            heads = dm // HEAD_DIM
            qkv = qkv_proj(x_bf, moba_w_qkv, j, rope_c, rope_s, heads * HEAD_DIM)
            att = moba_attention(qkv.reshape(b, s, -1), heads)
            w_o = moba_w_o
        z = proj_resid(att.reshape(n, -1), w_o, j, xf)
        xf, x_bf = moe_ln(z, ln_mix_g[i], ln_mix_b[i], router_w[i], router_b[i],
                          exp_w_gate, exp_b_gate[i], exp_w_up, exp_b_up[i], exp_w_down, i,
                          exp_b_down[i],
                          ln_ffn_g[i], ln_ffn_b[i])
    return xf.reshape(b, s, dm)
```
